```python
import jax, jax.numpy as jnp
from jax import lax
import numpy as np

D_MODEL = 1024
BATCH = 32
SEQ = 2048
DEPTH = 1

CHUNK = 64
Q_BLOCK = 2 * CHUNK
HEAD_DIM = 64
MIX_WIDTH = D_MODEL
RWKV_WIDTH = MIX_WIDTH // 2
SB_WIDTH = MIX_WIDTH - RWKV_WIDTH
RWKV_HEADS = RWKV_WIDTH // HEAD_DIM
SB_HEADS = SB_WIDTH // HEAD_DIM
DECAY_LORA = 64
AAA_LORA = 64
GATE_LORA = 128
D_FF = 4 * D_MODEL
RMS_EPS = 1e-5
GN_EPS = 64e-5
RWKV_COLS = 3 * RWKV_WIDTH + DECAY_LORA + AAA_LORA + GATE_LORA
SB_COLS = 3 * SB_WIDTH
IN_COLS = RWKV_COLS + SB_COLS

kernel_name = "hybrid_rwkv7_stickbreak_block"


def _rmsnorm(x, g):
    xf = x.astype(jnp.float32)
    y = xf * lax.rsqrt(jnp.mean(xf * xf, axis=-1, keepdims=True) + RMS_EPS)
    return (y * g.astype(jnp.float32)).astype(x.dtype)


def _token_shift(p):
    return jnp.pad(p[:, :-1], ((0, 0), (1, 0), (0, 0)))


def _rwkv7_time_mix(p, mu, w0, w_decay_up, a0, w_aaa_up, w_gate_up, k_k, k_a, r_k, gn_w, gn_b):
    B, T, _ = p.shape
    H, dh = RWKV_HEADS, HEAD_DIM
    p = p + mu * (_token_shift(p) - p)
    W = RWKV_WIDTH
    splits = [W, 2 * W, 3 * W, 3 * W + DECAY_LORA, 3 * W + DECAY_LORA + AAA_LORA]
    r, k, v, xw, xa, xg = jnp.split(p, splits, axis=-1)
    w = -jax.nn.softplus(-(w0 + jnp.tanh(xw) @ w_decay_up)) - 0.5
    a = jax.nn.sigmoid(a0 + xa @ w_aaa_up)
    g = jax.nn.sigmoid(xg) @ w_gate_up

    def heads(t):
        return t.reshape(B, T, H, dh).astype(jnp.float32)

    r, w, k, v, a = heads(r), heads(w), heads(k), heads(v), heads(a)
    kk = k * k_k.reshape(H, dh).astype(jnp.float32)
    kk = kk / jnp.maximum(jnp.sqrt(jnp.sum(kk * kk, axis=-1, keepdims=True)), 1e-12)
    k = k * (1.0 + (a - 1.0) * k_a.reshape(H, dh).astype(jnp.float32))
    decay = jnp.exp(-jnp.exp(w))

    def step(S, inp):
        r_t, d_t, k_t, v_t, a_t, b_t = inp
        sa = jnp.einsum('bhvk,bhk->bhv', S, a_t)
        S = S * d_t[:, :, None, :] + sa[..., None] * b_t[:, :, None, :] + v_t[..., None] * k_t[:, :, None, :]
        return S, jnp.einsum('bhvk,bhk->bhv', S, r_t)

    def seq_first(t):
        return jnp.swapaxes(t, 0, 1)

    S0 = jnp.zeros((B, H, dh, dh), jnp.float32)
    xs = (seq_first(r), seq_first(decay), seq_first(k), seq_first(v), seq_first(-kk), seq_first(kk * a))
    _, y = lax.scan(step, S0, xs)
    y = jnp.swapaxes(y, 0, 1)
    mean = jnp.mean(y, axis=-1, keepdims=True)
    var = jnp.mean(jnp.square(y - mean), axis=-1, keepdims=True)
    y = (y - mean) * lax.rsqrt(var + GN_EPS) * gn_w.reshape(H, dh).astype(jnp.float32) + gn_b.reshape(H, dh).astype(jnp.float32)
    bonus = jnp.sum(r * k * r_k.astype(jnp.float32), axis=-1, keepdims=True) * v
    out = (y + bonus).reshape(B, T, RWKV_WIDTH) * g.astype(jnp.float32)
    return out.astype(p.dtype)


def _stick_breaking_attention(p, sb_gain):
    B, T, _ = p.shape
    H, dh = SB_HEADS, HEAD_DIM
    q, k, v = jnp.split(p, 3, axis=-1)

    def to_heads(t):
        return t.reshape(B, T, H, dh).transpose(0, 2, 1, 3)

    q, k, v = to_heads(q), to_heads(k), to_heads(v)
    scale = HEAD_DIM ** -0.5
    outs = []
    for blk in range(T // Q_BLOCK):
        start = blk * Q_BLOCK
        end = start + Q_BLOCK
        z = jnp.einsum('bhqd,bhkd->bhqk', q[:, :, start:end], k[:, :, :end]).astype(jnp.float32) * scale
        causal = jnp.arange(end)[None, :] < (start + jnp.arange(Q_BLOCK))[:, None]
        log_beta = jax.nn.log_sigmoid(z)
        log_keep = jnp.where(causal, log_beta - z, 0.0)
        log_a = log_beta + lax.cumsum(log_keep, axis=3, reverse=True) - log_keep
        att = jnp.where(causal, jnp.exp(log_a), 0.0)
        outs.append(jnp.einsum('bhqk,bhkd->bhqd', att.astype(v.dtype), v[:, :, :end]))
    o = jnp.concatenate(outs, axis=2).transpose(0, 2, 1, 3).astype(jnp.float32)
    o = o * lax.rsqrt(jnp.mean(o * o, axis=-1, keepdims=True) + RMS_EPS) * sb_gain.reshape(H, dh).astype(jnp.float32)
    return o.reshape(B, T, SB_WIDTH).astype(p.dtype)


def setup_inputs(seed: int = 0) -> dict:
    key = jax.random.key(seed)
    ks = jax.random.split(key, 21)
    n = jax.random.normal
    f32 = jnp.float32
    return {
        "x": n(ks[0], (BATCH, SEQ, D_MODEL), f32),
        "ln1_g": 1.0 + 0.02 * n(ks[1], (DEPTH, D_MODEL), f32),
        "w_in": n(ks[2], (DEPTH, D_MODEL, IN_COLS), f32) * D_MODEL ** -0.5,
        "tok_mu": jax.random.uniform(ks[3], (DEPTH, RWKV_COLS), f32),
        "w0": jax.random.uniform(ks[4], (DEPTH, RWKV_WIDTH), f32, -6.0, -1.0),
        "w_decay_up": n(ks[5], (DEPTH, DECAY_LORA, RWKV_WIDTH), f32) * 0.5 * DECAY_LORA ** -0.5,
        "a0": 0.1 * n(ks[6], (DEPTH, RWKV_WIDTH), f32),
        "w_aaa_up": n(ks[7], (DEPTH, AAA_LORA, RWKV_WIDTH), f32) * 0.5 * AAA_LORA ** -0.5,
        "w_gate_up": n(ks[8], (DEPTH, GATE_LORA, RWKV_WIDTH), f32) * GATE_LORA ** -0.5,
        "k_k": 0.85 + 0.05 * n(ks[9], (DEPTH, RWKV_WIDTH), f32),
        "k_a": 1.0 + 0.05 * n(ks[10], (DEPTH, RWKV_WIDTH), f32),
        "r_k": 0.1 * n(ks[11], (DEPTH, RWKV_HEADS, HEAD_DIM), f32),
        "gn_w": 1.0 + 0.02 * n(ks[12], (DEPTH, RWKV_WIDTH), f32),
        "gn_b": 0.02 * n(ks[13], (DEPTH, RWKV_WIDTH), f32),
        "sb_gain": 1.0 + 0.02 * n(ks[14], (DEPTH, SB_WIDTH), f32),
        "w_out": n(ks[15], (DEPTH, MIX_WIDTH, D_MODEL), f32) * MIX_WIDTH ** -0.5,
        "ln2_g": 1.0 + 0.02 * n(ks[16], (DEPTH, D_MODEL), f32),
        "w_up": n(ks[17], (DEPTH, D_MODEL, D_FF), f32) * D_MODEL ** -0.5,
        "w_down": n(ks[18], (DEPTH, D_FF, D_MODEL), f32) * 0.5 * D_FF ** -0.5,
        "lnf_g": 1.0 + 0.02 * n(ks[19], (D_MODEL,), f32),
    }


def reference(x, ln1_g, w_in, tok_mu, w0, w_decay_up, a0, w_aaa_up, w_gate_up, k_k, k_a, r_k,
              gn_w, gn_b, sb_gain, w_out, ln2_g, w_up, w_down, lnf_g):
    for l in range(DEPTH):
        h = _rmsnorm(x, ln1_g[l])
        p = h @ w_in[l]
        y_rwkv = _rwkv7_time_mix(p[..., :RWKV_COLS], tok_mu[l], w0[l], w_decay_up[l], a0[l],
                                 w_aaa_up[l], w_gate_up[l], k_k[l], k_a[l], r_k[l], gn_w[l], gn_b[l])
        y_sb = _stick_breaking_attention(p[..., RWKV_COLS:], sb_gain[l])
        x = x + jnp.concatenate([y_rwkv, y_sb], axis=-1) @ w_out[l]
        h = _rmsnorm(x, ln2_g[l])
        x = x + jnp.square(jax.nn.relu(h @ w_up[l])) @ w_down[l]
    return _rmsnorm(x, lnf_g)
```

```python
import functools

import jax
import jax.numpy as jnp
from jax import lax
from jax.experimental import pallas as pl
from jax.experimental.pallas import tpu as pltpu

D_MODEL = 1024
HEAD_DIM = 64
RWKV_WIDTH = 512
SB_WIDTH = 512
DECAY_LORA = 64
AAA_LORA = 64
GATE_LORA = 128
LORA_IN = DECAY_LORA + AAA_LORA + GATE_LORA
RWKV_COLS = 3 * RWKV_WIDTH + LORA_IN
SB_COLS = 3 * SB_WIDTH
IN_COLS = RWKV_COLS + SB_COLS
D_FF = 4 * D_MODEL
RMS_EPS = 1e-5
GN_EPS = 64e-5
KK_NORM_FLOOR = 1e-12

LANES = 128
PAIRS = RWKV_WIDTH // LANES
CHUNK = 64
Q_BLOCK = 128
ROW_TILE = 512
FF_CHUNK = 1024
RWKV_SEQS = 2
EXP_UNDERFLOW = -104.0
VMEM_LIMIT = 56 * 1024 * 1024

F32 = jnp.float32
BF16 = jnp.bfloat16


def _bf(x):
    return x.astype(BF16)


def _dot(a, b):
    return jnp.dot(a, b, preferred_element_type=F32)


def _dot_nt(a, b):
    return lax.dot_general(a, b, (((1,), (1,)), ((), ())), preferred_element_type=F32)


def _split(x):
    hi = _bf(x)
    lo = _bf(x - hi.astype(F32))
    return hi, lo


def _dot_split_lhs(x, w):
    hi, lo = _split(x)
    return _dot(hi, w) + _dot(lo, w)


def _iota(shape, axis):
    return lax.broadcasted_iota(jnp.int32, shape, axis)


def _head_block_matrix(value):
    same = (_iota((LANES, LANES), 0) >> 6) == (_iota((LANES, LANES), 1) >> 6)
    return jnp.where(same, value, 0.0).astype(BF16)


def _head_sum(x, block):
    cols = [
        _dot_split_lhs(x[:, j * LANES:(j + 1) * LANES], block)
        for j in range(x.shape[1] // LANES)
    ]
    return cols[0] if len(cols) == 1 else jnp.concatenate(cols, axis=1)


def _softplus(x):
    return jnp.maximum(x, 0.0) + jnp.log(1.0 + jnp.exp(-jnp.abs(x)))


def _sigmoid(x):
    return 1.0 / (1.0 + jnp.exp(-x))


def _rms(x):
    return x * lax.rsqrt(jnp.mean(x * x, axis=-1, keepdims=True) + RMS_EPS)


def _inproj_kernel(x_ref, g_ref, w_ref, pr_ref, psb_ref):
    h = _bf(_rms(x_ref[...]) * g_ref[...])
    pr_ref[...] = _bf(_dot(h, w_ref[:, :RWKV_COLS]))
    psb_ref[...] = _bf(_dot(h, w_ref[:, RWKV_COLS:]))


def _inproj(x2d, g, w_bf):
    n = x2d.shape[0]
    return pl.pallas_call(
        _inproj_kernel,
        grid=(n // ROW_TILE,),
        in_specs=[
            pl.BlockSpec((ROW_TILE, D_MODEL), lambda i: (i, 0)),
            pl.BlockSpec((1, D_MODEL), lambda i: (0, 0)),
            pl.BlockSpec((D_MODEL, IN_COLS), lambda i: (0, 0)),
        ],
        out_specs=[
            pl.BlockSpec((ROW_TILE, RWKV_COLS), lambda i: (i, 0)),
            pl.BlockSpec((ROW_TILE, SB_COLS), lambda i: (i, 0)),
        ],
        out_shape=[
            jax.ShapeDtypeStruct((n, RWKV_COLS), BF16),
            jax.ShapeDtypeStruct((n, SB_COLS), BF16),
        ],
        compiler_params=pltpu.CompilerParams(
            dimension_semantics=("arbitrary",), vmem_limit_bytes=VMEM_LIMIT),
        name="inproj",
    )(x2d, g, w_bf)


def _rwkv_pair(at, rt, bt, kt, bd_, kd_, v, dl, s, m):
    def blockdiag(y):
        return _bf(jnp.concatenate(
            [jnp.where(m["lane_lo"], y, 0.0), jnp.where(m["lane_lo"], 0.0, y)], axis=0))

    g = _dot_nt(_bf(jnp.concatenate([at, rt], axis=0)),
                jnp.concatenate([blockdiag(bt), blockdiag(kt)], axis=0))
    a_ab = jnp.where(m["strict"], g[:CHUNK, :LANES], 0.0)
    a_ak = jnp.where(m["strict"], g[:CHUNK, LANES:], 0.0)
    a_rb = jnp.where(m["incl"], g[CHUNK:, :LANES], 0.0)
    a_rk = jnp.where(m["incl"], g[CHUNK:, LANES:], 0.0)

    t = m["eye"] + a_ab
    x = _dot(_bf(a_ab), blockdiag(a_ab))
    for _ in range(4):
        tx = _dot(_bf(jnp.concatenate([t, x], axis=0)), blockdiag(x))
        t = t + tx[:CHUNK]
        x = tx[CHUNK:]
    t = t + _dot(_bf(t), blockdiag(x))

    v_bd = blockdiag(v)
    av = _dot(_bf(a_ak), v_bd)
    wu = _dot(_bf(t), jnp.concatenate([blockdiag(at), blockdiag(av)], axis=1))
    w = wu[:, :LANES]
    u0 = wu[:, LANES:]
    ry = _dot(_bf(a_rb), jnp.concatenate([blockdiag(w), blockdiag(u0)], axis=1))
    rhat = rt + ry[:, :LANES]
    y0 = ry[:, LANES:] + _dot(_bf(a_rk), v_bd)

    bk_t = jnp.concatenate([bd_, kd_], axis=0).T
    rhs = jnp.concatenate(
        [jnp.concatenate([w, u0], axis=1),
         jnp.concatenate([jnp.zeros_like(v), v], axis=1)], axis=0)
    mn = _dot(_bf(bk_t), _bf(rhs))
    m_bd = jnp.where(m["same_head"], mn[:, :LANES], 0.0) + jnp.where(m["diag"], dl, 0.0)
    n_bd = jnp.where(m["same_head"], mn[:, LANES:], 0.0)

    rs = _dot(_bf(jnp.concatenate([rhat, m_bd], axis=0)), _bf(s))
    return rs[:CHUNK] + y0, rs[CHUNK:] + n_bd


def _rwkv_kernel(p_ref, mu_ref, w0_ref, a0_ref, kk_ref, ka_ref, rk_ref, gw_ref, gb_ref, wl_ref,
                 y_ref, carry_ref, state_ref, *, seqs):
    @pl.when(pl.program_id(1) == 0)
    def _():
        carry_ref[...] = jnp.zeros_like(carry_ref)
        state_ref[...] = jnp.zeros_like(state_ref)

    rows = seqs * CHUNK
    w = RWKV_WIDTH
    first_row = _iota((8, RWKV_COLS), 0) == 0

    p_parts, prev_parts = [], []
    for b in range(seqs):
        pb = p_ref[b].astype(F32)
        rolled = pltpu.roll(pb, 1, 0)
        head = jnp.where(first_row, carry_ref[b], rolled[:8])
        prev_parts.append(jnp.concatenate([head, rolled[8:]], axis=0))
        carry_ref[b] = pb[CHUNK - 1:CHUNK]
        p_parts.append(pb)
    p = jnp.concatenate(p_parts, axis=0)
    prev = jnp.concatenate(prev_parts, axis=0)
    p = p + mu_ref[...] * (prev - p)

    r = p[:, :w]
    k = p[:, w:2 * w]
    v = p[:, 2 * w:3 * w]
    xl = p[:, 3 * w:]
    lane_l = _iota((rows, LORA_IN), 1)
    lora_in = jnp.where(lane_l < DECAY_LORA, jnp.tanh(xl),
                        jnp.where(lane_l < DECAY_LORA + AAA_LORA, xl, _sigmoid(xl)))
    lora = _dot(_bf(lora_in), wl_ref[...])
    wlog = -_softplus(-(w0_ref[...] + lora[:, :w])) - 0.5
    logd = -jnp.exp(wlog)
    asig = _sigmoid(a0_ref[...] + lora[:, w:2 * w])
    gate = lora[:, 2 * w:]

    ones_blk = _head_block_matrix(1.0)
    mean_blk = _head_block_matrix(1.0 / HEAD_DIM)
    kkr = k * kk_ref[...]
    kk = kkr / jnp.maximum(jnp.sqrt(_head_sum(kkr * kkr, ones_blk)), KK_NORM_FLOOR)
    kmod = k * (1.0 + (asig - 1.0) * ka_ref[...])
    a_vec = -kk
    b_vec = kk * asig
    bonus = _head_sum(r * kmod * rk_ref[...], ones_blk) * v

    row_t = _iota((CHUNK, LANES), 0)
    col_i = _iota((CHUNK, LANES), 1) & (HEAD_DIM - 1)
    row_l = _iota((LANES, LANES), 0)
    col_l = _iota((LANES, LANES), 1)
    masks = {
        "lane_lo": _iota((CHUNK, LANES), 1) < HEAD_DIM,
        "strict": row_t > col_i,
        "incl": row_t >= col_i,
        "eye": jnp.where(row_t == col_i, 1.0, 0.0).astype(F32),
        "same_head": (row_l >> 6) == (col_l >> 6),
        "diag": row_l == col_l,
    }
    tri = jnp.where(_iota((CHUNK, CHUNK), 0) >= _iota((CHUNK, CHUNK), 1), 1.0, 0.0).astype(BF16)

    y_parts = []
    for b in range(seqs):
        sl = slice(b * CHUNK, (b + 1) * CHUNK)
        ld = logd[sl]
        hi, lo = _split(ld)
        cl = _dot(tri, hi) + _dot(tri, lo)
        e_pos = jnp.exp(cl)
        e_neg = jnp.exp(-cl)
        e_prev = jnp.exp(cl - ld)
        dl = e_pos[CHUNK - 1:CHUNK]
        at = a_vec[sl] * e_prev
        rt = r[sl] * e_pos
        bt = b_vec[sl] * e_neg
        kt = kmod[sl] * e_neg
        bd_ = bt * dl
        kd_ = kt * dl
        vb = v[sl]
        y_pairs = []
        for pp in range(PAIRS):
            ls = slice(pp * LANES, (pp + 1) * LANES)
            y_pair, s_new = _rwkv_pair(at[:, ls], rt[:, ls], bt[:, ls], kt[:, ls], bd_[:, ls],
                                       kd_[:, ls], vb[:, ls], dl[:, ls], state_ref[b, pp], masks)
            state_ref[b, pp] = s_new
            y_pairs.append(y_pair)
        y_parts.append(jnp.concatenate(y_pairs, axis=1))
    y = jnp.concatenate(y_parts, axis=0)

    mean = _head_sum(y, mean_blk)
    yc = y - mean
    var = _head_sum(yc * yc, mean_blk)
    yn = yc * lax.rsqrt(var + GN_EPS) * gw_ref[...] + gb_ref[...]
    out = _bf((yn + bonus) * gate)
    for b in range(seqs):
        y_ref[b] = out[b * CHUNK:(b + 1) * CHUNK]


def _rwkv(pr, mu, w0, a0, k_k, k_a, r_k, gn_w, gn_b, w_lora):
    bsz, t, _ = pr.shape
    seqs = RWKV_SEQS if bsz % RWKV_SEQS == 0 else 1
    vec = lambda c: pl.BlockSpec((1, c), lambda b, j: (0, 0))
    return pl.pallas_call(
        functools.partial(_rwkv_kernel, seqs=seqs),
        grid=(bsz // seqs, t // CHUNK),
        in_specs=[
            pl.BlockSpec((seqs, CHUNK, RWKV_COLS), lambda b, j: (b, j, 0)),
            vec(RWKV_COLS), vec(RWKV_WIDTH), vec(RWKV_WIDTH), vec(RWKV_WIDTH), vec(RWKV_WIDTH),
            vec(RWKV_WIDTH), vec(RWKV_WIDTH), vec(RWKV_WIDTH),
            pl.BlockSpec((LORA_IN, 3 * RWKV_WIDTH), lambda b, j: (0, 0)),
        ],
        out_specs=pl.BlockSpec((seqs, CHUNK, RWKV_WIDTH), lambda b, j: (b, j, 0)),
        out_shape=jax.ShapeDtypeStruct((bsz, t, RWKV_WIDTH), BF16),
        scratch_shapes=[
            pltpu.VMEM((seqs, 1, RWKV_COLS), F32),
            pltpu.VMEM((seqs, PAIRS, LANES, LANES), F32),
        ],
        compiler_params=pltpu.CompilerParams(
            dimension_semantics=("arbitrary", "arbitrary"), vmem_limit_bytes=VMEM_LIMIT),
        name="rwkv7",
    )(pr, mu, w0, a0, k_k, k_a, r_k, gn_w, gn_b, w_lora)


def _sb_kernel(q_ref, k_ref, v_ref, gain_ref, o_ref, c_ref, acc_ref):
    qi = pl.program_id(2)
    rows2 = 2 * Q_BLOCK
    lane_lo = _iota((Q_BLOCK, LANES), 1) < HEAD_DIM
    q = q_ref[...]
    zero = jnp.zeros_like(q)
    q2 = jnp.concatenate([jnp.where(lane_lo, q, zero), jnp.where(lane_lo, zero, q)], axis=0)
    q2 = q2 * jnp.asarray(HEAD_DIM ** -0.5, BF16)

    row_q = _iota((rows2, LANES), 0) & (Q_BLOCK - 1)
    col_k = _iota((rows2, LANES), 1)
    cum = jnp.where(
        (_iota((LANES, 2 * LANES), 0) > _iota((LANES, 2 * LANES), 1))
        | (_iota((LANES, 2 * LANES), 1) >= LANES), 1.0, 0.0).astype(BF16)

    c_ref[...] = jnp.zeros_like(c_ref)
    acc_ref[...] = jnp.zeros_like(acc_ref)

    def cond(carry):
        j, cmax = carry
        return jnp.logical_and(j >= 0, cmax >= EXP_UNDERFLOW)

    def body(carry):
        j, _ = carry
        start = pl.multiple_of(j * Q_BLOCK, Q_BLOCK)
        kj = k_ref[pl.ds(start, Q_BLOCK), :]
        vj = v_ref[pl.ds(start, Q_BLOCK), :]
        z = _dot_nt(q2, kj)
        sp = _softplus(z)
        log_beta = z - sp
        causal = (col_k + (j - qi) * Q_BLOCK) < row_q
        log_keep = jnp.where(causal, -sp, 0.0)
        hi, lo = _split(log_keep)
        cs = _dot(hi, cum) + _dot(lo, cum)
        c_old = c_ref[...]
        att = jnp.where(causal, jnp.exp(log_beta + cs[:, :LANES] + c_old), 0.0)
        acc_ref[...] += _dot(_bf(att), vj)
        c_new = c_old + cs[:, LANES:]
        c_ref[...] = c_new
        return j - 1, jnp.max(c_new)

    lax.while_loop(cond, body, (qi, jnp.float32(0.0)))

    acc = acc_ref[...]
    o = jnp.where(lane_lo, acc[:Q_BLOCK], acc[Q_BLOCK:])
    ms = _head_sum(o * o, _head_block_matrix(1.0 / HEAD_DIM))
    o_ref[...] = _bf(o * lax.rsqrt(ms + RMS_EPS) * gain_ref[...])


def _sb(psb, gain):
    bsz, t, _ = psb.shape
    return pl.pallas_call(
        _sb_kernel,
        grid=(bsz, PAIRS, t // Q_BLOCK),
        in_specs=[
            pl.BlockSpec((None, Q_BLOCK, LANES), lambda b, p, i: (b, i, p)),
            pl.BlockSpec((None, t, LANES), lambda b, p, i: (b, 0, PAIRS + p)),
            pl.BlockSpec((None, t, LANES), lambda b, p, i: (b, 0, 2 * PAIRS + p)),
            pl.BlockSpec((1, LANES), lambda b, p, i: (0, p)),
        ],
        out_specs=pl.BlockSpec((None, Q_BLOCK, LANES), lambda b, p, i: (b, i, p)),
        out_shape=jax.ShapeDtypeStruct((bsz, t, SB_WIDTH), BF16),
        scratch_shapes=[
            pltpu.VMEM((2 * Q_BLOCK, LANES), F32),
            pltpu.VMEM((2 * Q_BLOCK, LANES), F32),
        ],
        compiler_params=pltpu.CompilerParams(
            dimension_semantics=("arbitrary", "arbitrary", "arbitrary"),
            vmem_limit_bytes=VMEM_LIMIT),
        name="stickbreak",
    )(psb, psb, psb, gain)


def _out_mlp_kernel(x_ref, ya_ref, yb_ref, wo_ref, g2_ref, wu_ref, wd_ref, gf_ref, o_ref, *, final):
    x1 = (x_ref[...] + _dot(ya_ref[...], wo_ref[:RWKV_WIDTH, :])
          + _dot(yb_ref[...], wo_ref[RWKV_WIDTH:, :]))
    h = _bf(_rms(x1) * g2_ref[...])
    o_ref[...] = x1
    for c in range(D_FF // FF_CHUNK):
        cs = slice(c * FF_CHUNK, (c + 1) * FF_CHUNK)
        u = jnp.maximum(_dot(h, wu_ref[:, cs]), 0.0)
        o_ref[...] += _dot(_bf(u * u), wd_ref[cs, :])
    if final:
        o_ref[...] = _rms(o_ref[...]) * gf_ref[...]


def _out_mlp(x2d, ya, yb, wo, g2, wu, wd, gf, final):
    n = x2d.shape[0]
    const = lambda shape: pl.BlockSpec(shape, lambda i: (0, 0), pipeline_mode=pl.Buffered(1))
    return pl.pallas_call(
        functools.partial(_out_mlp_kernel, final=final),
        grid=(n // ROW_TILE,),
        in_specs=[
            pl.BlockSpec((ROW_TILE, D_MODEL), lambda i: (i, 0)),
            pl.BlockSpec((ROW_TILE, RWKV_WIDTH), lambda i: (i, 0)),
            pl.BlockSpec((ROW_TILE, SB_WIDTH), lambda i: (i, 0)),
            const((D_MODEL, D_MODEL)),
            const((1, D_MODEL)),
            const((D_MODEL, D_FF)),
            const((D_FF, D_MODEL)),
            const((1, D_MODEL)),
        ],
        out_specs=pl.BlockSpec((ROW_TILE, D_MODEL), lambda i: (i, 0)),
        out_shape=jax.ShapeDtypeStruct((n, D_MODEL), F32),
        compiler_params=pltpu.CompilerParams(
            dimension_semantics=("arbitrary",), vmem_limit_bytes=VMEM_LIMIT),
        name="outproj_mlp",
    )(x2d, ya, yb, wo, g2, wu, wd, gf)


def _lora_weight(w_decay_up, w_aaa_up, w_gate_up):
    z = lambda r: jnp.zeros((r, RWKV_WIDTH), F32)
    return _bf(jnp.concatenate([
        jnp.concatenate([w_decay_up, z(DECAY_LORA), z(DECAY_LORA)], axis=1),
        jnp.concatenate([z(AAA_LORA), w_aaa_up, z(AAA_LORA)], axis=1),
        jnp.concatenate([z(GATE_LORA), z(GATE_LORA), w_gate_up], axis=1),
    ], axis=0))


def kernel(x, ln1_g, w_in, tok_mu, w0, w_decay_up, a0, w_aaa_up, w_gate_up, k_k, k_a, r_k,
           gn_w, gn_b, sb_gain, w_out, ln2_g, w_up, w_down, lnf_g):
    bsz, t, d = x.shape
    assert d == D_MODEL and t % Q_BLOCK == 0 and (bsz * t) % ROW_TILE == 0
    depth = ln1_g.shape[0]
    row = lambda a: a.reshape(1, -1).astype(F32)
    x2d = x.reshape(bsz * t, d)
    for l in range(depth):
        pr, psb = _inproj(x2d, row(ln1_g[l]), _bf(w_in[l]))
        y_rwkv = _rwkv(pr.reshape(bsz, t, RWKV_COLS), row(tok_mu[l]), row(w0[l]), row(a0[l]),
                       row(k_k[l]), row(k_a[l]), row(r_k[l]), row(gn_w[l]), row(gn_b[l]),
                       _lora_weight(w_decay_up[l], w_aaa_up[l], w_gate_up[l]))
        y_sb = _sb(psb.reshape(bsz, t, SB_COLS), row(sb_gain[l]))
        x2d = _out_mlp(x2d, y_rwkv.reshape(bsz * t, RWKV_WIDTH), y_sb.reshape(bsz * t, SB_WIDTH),
                       _bf(w_out[l]), row(ln2_g[l]), _bf(w_up[l]), _bf(w_down[l]), row(lnf_g),
                       final=(l == depth - 1))
    return x2d.reshape(bsz, t, d)
```

```python
import functools

import jax
import jax.numpy as jnp
from jax import lax
from jax.experimental import pallas as pl
from jax.experimental.pallas import tpu as pltpu

D_MODEL = 1024
HEAD_DIM = 64
RWKV_WIDTH = 512
SB_WIDTH = 512
DECAY_LORA = 64
AAA_LORA = 64
GATE_LORA = 128
LORA_IN = DECAY_LORA + AAA_LORA + GATE_LORA
RWKV_COLS = 3 * RWKV_WIDTH + LORA_IN
SB_COLS = 3 * SB_WIDTH
IN_COLS = RWKV_COLS + SB_COLS
D_FF = 4 * D_MODEL
RMS_EPS = 1e-5
GN_EPS = 64e-5
KK_NORM_FLOOR = 1e-12

LANES = 128
PAIRS = RWKV_WIDTH // LANES
CHUNK = 64
Q_BLOCK = 128
ROW_TILE = 512
FF_CHUNK = 1024
RWKV_SEQS = 4
EXP_UNDERFLOW = -104.0
VMEM_LIMIT = 56 * 1024 * 1024

F32 = jnp.float32
BF16 = jnp.bfloat16


def _bf(x):
    return x.astype(BF16)


def _dot(a, b):
    return jnp.dot(a, b, preferred_element_type=F32)


def _dot_nt(a, b):
    return lax.dot_general(a, b, (((1,), (1,)), ((), ())), preferred_element_type=F32)


def _split(x):
    hi = _bf(x)
    lo = _bf(x - hi.astype(F32))
    return hi, lo


def _dot_split_lhs(x, w):
    hi, lo = _split(x)
    return _dot(hi, w) + _dot(lo, w)


def _iota(shape, axis):
    return lax.broadcasted_iota(jnp.int32, shape, axis)


def _head_block_matrix(value):
    same = (_iota((LANES, LANES), 0) >> 6) == (_iota((LANES, LANES), 1) >> 6)
    return jnp.where(same, value, 0.0).astype(BF16)


def _head_sum(x, block):
    cols = [
        _dot_split_lhs(x[:, j * LANES:(j + 1) * LANES], block)
        for j in range(x.shape[1] // LANES)
    ]
    return cols[0] if len(cols) == 1 else jnp.concatenate(cols, axis=1)


def _softplus(x):
    return jnp.maximum(x, 0.0) + jnp.log(1.0 + jnp.exp(-jnp.abs(x)))


def _sigmoid(x):
    return 1.0 / (1.0 + jnp.exp(-x))


def _rms(x):
    return x * lax.rsqrt(jnp.mean(x * x, axis=-1, keepdims=True) + RMS_EPS)


def _inproj_kernel(x_ref, g_ref, w_ref, pr_ref, psb_ref):
    h = _bf(_rms(x_ref[...]) * g_ref[...])
    pr_ref[...] = _bf(_dot(h, w_ref[:, :RWKV_COLS]))
    psb_ref[...] = _bf(_dot(h, w_ref[:, RWKV_COLS:]))


def _inproj(x2d, g, w_bf):
    n = x2d.shape[0]
    return pl.pallas_call(
        _inproj_kernel,
        grid=(n // ROW_TILE,),
        in_specs=[
            pl.BlockSpec((ROW_TILE, D_MODEL), lambda i: (i, 0)),
            pl.BlockSpec((1, D_MODEL), lambda i: (0, 0)),
            pl.BlockSpec((D_MODEL, IN_COLS), lambda i: (0, 0)),
        ],
        out_specs=[
            pl.BlockSpec((ROW_TILE, RWKV_COLS), lambda i: (i, 0)),
            pl.BlockSpec((ROW_TILE, SB_COLS), lambda i: (i, 0)),
        ],
        out_shape=[
            jax.ShapeDtypeStruct((n, RWKV_COLS), BF16),
            jax.ShapeDtypeStruct((n, SB_COLS), BF16),
        ],
        compiler_params=pltpu.CompilerParams(
            dimension_semantics=("arbitrary",), vmem_limit_bytes=VMEM_LIMIT),
        name="inproj",
    )(x2d, g, w_bf)


def _rwkv_chains(ch, m):
    def blockdiag(y):
        return _bf(jnp.concatenate(
            [jnp.where(m["lane_lo"], y, 0.0), jnp.where(m["lane_lo"], 0.0, y)], axis=0))

    def each(fn, *lists):
        return [fn(*xs) for xs in zip(*lists)]

    at, rt, bt, kt = ([c[k] for c in ch] for k in ("at", "rt", "bt", "kt"))
    v, dl, s = ([c[k] for c in ch] for k in ("v", "dl", "s"))
    g = each(lambda a, r, b, k: _dot_nt(_bf(jnp.concatenate([a, r], axis=0)),
                                        jnp.concatenate([blockdiag(b), blockdiag(k)], axis=0)),
             at, rt, bt, kt)
    a_ab = [jnp.where(m["strict"], x[:CHUNK, :LANES], 0.0) for x in g]
    a_ak = [jnp.where(m["strict"], x[:CHUNK, LANES:], 0.0) for x in g]
    a_rb = [jnp.where(m["incl"], x[CHUNK:, :LANES], 0.0) for x in g]
    a_rk = [jnp.where(m["incl"], x[CHUNK:, LANES:], 0.0) for x in g]

    t = [m["eye"] + a for a in a_ab]
    x = each(lambda a: _dot(_bf(a), blockdiag(a)), a_ab)
    for _ in range(4):
        tx = each(lambda ti, xi: _dot(_bf(jnp.concatenate([ti, xi], axis=0)), blockdiag(xi)), t, x)
        t = each(lambda ti, txi: ti + txi[:CHUNK], t, tx)
        x = [txi[CHUNK:] for txi in tx]
    t = each(lambda ti, xi: ti + _dot(_bf(ti), blockdiag(xi)), t, x)

    v_bd = [blockdiag(vi) for vi in v]
    av = each(lambda a, vb: _dot(_bf(a), vb), a_ak, v_bd)
    wu = each(lambda ti, a, avi: _dot(_bf(ti), jnp.concatenate([blockdiag(a), blockdiag(avi)], axis=1)),
              t, at, av)
    w = [x_[:, :LANES] for x_ in wu]
    u0 = [x_[:, LANES:] for x_ in wu]
    ry = each(lambda a, wi, ui: _dot(_bf(a), jnp.concatenate([blockdiag(wi), blockdiag(ui)], axis=1)),
              a_rb, w, u0)
    yk = each(lambda a, vb: _dot(_bf(a), vb), a_rk, v_bd)
    rhat = each(lambda r, x_: r + x_[:, :LANES], rt, ry)
    y0 = each(lambda x_, k_: x_[:, LANES:] + k_, ry, yk)

    def transition(c, wi, ui, vi):
        bk_t = jnp.concatenate([c["bd"], c["kd"]], axis=0).T
        rhs = jnp.concatenate(
            [jnp.concatenate([wi, ui], axis=1),
             jnp.concatenate([jnp.zeros_like(vi), vi], axis=1)], axis=0)
        return _dot(_bf(bk_t), _bf(rhs))

    mn = each(transition, ch, w, u0, v)
    m_bd = each(lambda x_, d: jnp.where(m["same_head"], x_[:, :LANES], 0.0) + jnp.where(m["diag"], d, 0.0),
                mn, dl)
    n_bd = [jnp.where(m["same_head"], x_[:, LANES:], 0.0) for x_ in mn]
    rs = each(lambda r, mb, si: _dot(_bf(jnp.concatenate([r, mb], axis=0)), _bf(si)), rhat, m_bd, s)
    y = each(lambda x_, y_: x_[:CHUNK] + y_, rs, y0)
    s_new = each(lambda x_, n_: x_[CHUNK:] + n_, rs, n_bd)
    return y, s_new


def _rwkv_kernel(p_ref, mu_ref, w0_ref, a0_ref, kk_ref, ka_ref, rk_ref, gw_ref, gb_ref, wl_ref,
                 y_ref, carry_ref, state_ref, *, seqs):
    @pl.when(pl.program_id(1) == 0)
    def _():
        carry_ref[...] = jnp.zeros_like(carry_ref)
        state_ref[...] = jnp.zeros_like(state_ref)

    rows = seqs * CHUNK
    w = RWKV_WIDTH
    first_row = _iota((8, RWKV_COLS), 0) == 0

    p_parts, prev_parts = [], []
    for b in range(seqs):
        pb = p_ref[b].astype(F32)
        rolled = pltpu.roll(pb, 1, 0)
        head = jnp.where(first_row, carry_ref[b], rolled[:8])
        prev_parts.append(jnp.concatenate([head, rolled[8:]], axis=0))
        carry_ref[b] = pb[CHUNK - 1:CHUNK]
        p_parts.append(pb)
    p = jnp.concatenate(p_parts, axis=0)
    prev = jnp.concatenate(prev_parts, axis=0)
    p = p + mu_ref[...] * (prev - p)

    r = p[:, :w]
    k = p[:, w:2 * w]
    v = p[:, 2 * w:3 * w]
    xl = p[:, 3 * w:]
    lane_l = _iota((rows, LORA_IN), 1)
    lora_in = jnp.where(lane_l < DECAY_LORA, jnp.tanh(xl),
                        jnp.where(lane_l < DECAY_LORA + AAA_LORA, xl, _sigmoid(xl)))
    lora = _dot(_bf(lora_in), wl_ref[...])
    wlog = -_softplus(-(w0_ref[...] + lora[:, :w])) - 0.5
    logd = -jnp.exp(wlog)
    asig = _sigmoid(a0_ref[...] + lora[:, w:2 * w])
    gate = lora[:, 2 * w:]

    ones_blk = _head_block_matrix(1.0)
    mean_blk = _head_block_matrix(1.0 / HEAD_DIM)
    kkr = k * kk_ref[...]
    kk = kkr / jnp.maximum(jnp.sqrt(_head_sum(kkr * kkr, ones_blk)), KK_NORM_FLOOR)
    kmod = k * (1.0 + (asig - 1.0) * ka_ref[...])
    a_vec = -kk
    b_vec = kk * asig
    bonus = _head_sum(r * kmod * rk_ref[...], ones_blk) * v

    row_t = _iota((CHUNK, LANES), 0)
    col_i = _iota((CHUNK, LANES), 1) & (HEAD_DIM - 1)
    row_l = _iota((LANES, LANES), 0)
    col_l = _iota((LANES, LANES), 1)
    masks = {
        "lane_lo": _iota((CHUNK, LANES), 1) < HEAD_DIM,
        "strict": row_t > col_i,
        "incl": row_t >= col_i,
        "eye": jnp.where(row_t == col_i, 1.0, 0.0).astype(F32),
        "same_head": (row_l >> 6) == (col_l >> 6),
        "diag": row_l == col_l,
    }
    tri = jnp.where(_iota((CHUNK, CHUNK), 0) >= _iota((CHUNK, CHUNK), 1), 1.0, 0.0).astype(BF16)

    chains = []
    for b in range(seqs):
        sl = slice(b * CHUNK, (b + 1) * CHUNK)
        ld = logd[sl]
        hi, lo = _split(ld)
        cl = _dot(tri, hi) + _dot(tri, lo)
        e_pos = jnp.exp(cl)
        e_neg = jnp.exp(-cl)
        e_prev = jnp.exp(cl - ld)
        dl = e_pos[CHUNK - 1:CHUNK]
        at = a_vec[sl] * e_prev
        rt = r[sl] * e_pos
        bt = b_vec[sl] * e_neg
        kt = kmod[sl] * e_neg
        bd_ = bt * dl
        kd_ = kt * dl
        vb = v[sl]
        for pp in range(PAIRS):
            ls = slice(pp * LANES, (pp + 1) * LANES)
            chains.append(dict(at=at[:, ls], rt=rt[:, ls], bt=bt[:, ls], kt=kt[:, ls], bd=bd_[:, ls],
                               kd=kd_[:, ls], v=vb[:, ls], dl=dl[:, ls], s=state_ref[b, pp]))
    y_chains, s_chains = _rwkv_chains(chains, masks)
    for b in range(seqs):
        for pp in range(PAIRS):
            state_ref[b, pp] = s_chains[b * PAIRS + pp]
    y = jnp.concatenate(
        [jnp.concatenate(y_chains[b * PAIRS:(b + 1) * PAIRS], axis=1) for b in range(seqs)], axis=0)

    mean = _head_sum(y, mean_blk)
    yc = y - mean
    var = _head_sum(yc * yc, mean_blk)
    yn = yc * lax.rsqrt(var + GN_EPS) * gw_ref[...] + gb_ref[...]
    out = _bf((yn + bonus) * gate)
    for b in range(seqs):
        y_ref[b] = out[b * CHUNK:(b + 1) * CHUNK]


def _rwkv(pr, mu, w0, a0, k_k, k_a, r_k, gn_w, gn_b, w_lora):
    bsz, t, _ = pr.shape
    seqs = RWKV_SEQS if bsz % RWKV_SEQS == 0 else 1
    vec = lambda c: pl.BlockSpec((1, c), lambda b, j: (0, 0))
    return pl.pallas_call(
        functools.partial(_rwkv_kernel, seqs=seqs),
        grid=(bsz // seqs, t // CHUNK),
        in_specs=[
            pl.BlockSpec((seqs, CHUNK, RWKV_COLS), lambda b, j: (b, j, 0)),
            vec(RWKV_COLS), vec(RWKV_WIDTH), vec(RWKV_WIDTH), vec(RWKV_WIDTH), vec(RWKV_WIDTH),
            vec(RWKV_WIDTH), vec(RWKV_WIDTH), vec(RWKV_WIDTH),
            pl.BlockSpec((LORA_IN, 3 * RWKV_WIDTH), lambda b, j: (0, 0)),
        ],
        out_specs=pl.BlockSpec((seqs, CHUNK, RWKV_WIDTH), lambda b, j: (b, j, 0)),
        out_shape=jax.ShapeDtypeStruct((bsz, t, RWKV_WIDTH), BF16),
        scratch_shapes=[
            pltpu.VMEM((seqs, 1, RWKV_COLS), F32),
            pltpu.VMEM((seqs, PAIRS, LANES, LANES), F32),
        ],
        compiler_params=pltpu.CompilerParams(
            dimension_semantics=("arbitrary", "arbitrary"), vmem_limit_bytes=VMEM_LIMIT),
        name="rwkv7",
    )(pr, mu, w0, a0, k_k, k_a, r_k, gn_w, gn_b, w_lora)


def _sb_kernel(q_ref, k_ref, v_ref, gain_ref, o_ref, c_ref, acc_ref):
    qi = pl.program_id(1)
    rows2 = 2 * Q_BLOCK
    lane_lo = _iota((Q_BLOCK, LANES), 1) < HEAD_DIM
    scale = jnp.asarray(HEAD_DIM ** -0.5, BF16)
    q2 = []
    for u in range(PAIRS):
        q = q_ref[:, u * LANES:(u + 1) * LANES]
        zero = jnp.zeros_like(q)
        q2.append(jnp.concatenate([jnp.where(lane_lo, q, zero), jnp.where(lane_lo, zero, q)], axis=0) * scale)

    key_minus_query = _iota((rows2, LANES), 1) - (_iota((rows2, LANES), 0) & (Q_BLOCK - 1))
    cum = jnp.where(
        (_iota((LANES, 2 * LANES), 0) > _iota((LANES, 2 * LANES), 1))
        | (_iota((LANES, 2 * LANES), 1) >= LANES), 1.0, 0.0).astype(BF16)

    c_ref[...] = jnp.zeros_like(c_ref)
    acc_ref[...] = jnp.zeros_like(acc_ref)

    def cond(carry):
        j, cmax = carry
        return jnp.logical_and(j >= 0, cmax >= EXP_UNDERFLOW)

    def body(carry):
        j, _ = carry
        start = pl.multiple_of(j * Q_BLOCK, Q_BLOCK)
        kj = k_ref[pl.ds(start, Q_BLOCK), :]
        vj = v_ref[pl.ds(start, Q_BLOCK), :]
        causal = key_minus_query < (qi - j) * Q_BLOCK
        units = range(PAIRS)
        z = [_dot_nt(q2[u], kj[:, u * LANES:(u + 1) * LANES]) for u in units]
        sp = [_softplus(x) for x in z]
        parts = [_split(jnp.where(causal, -x, 0.0)) for x in sp]
        cs = [_dot(hi, cum) + _dot(lo, cum) for hi, lo in parts]
        c_old = [c_ref[u] for u in units]
        att = [_bf(jnp.where(causal, jnp.exp(z[u] - sp[u] + cs[u][:, :LANES] + c_old[u]), 0.0))
               for u in units]
        pv = [_dot(att[u], vj[:, u * LANES:(u + 1) * LANES]) for u in units]
        cmax = None
        for u in units:
            acc_ref[u] += pv[u]
            c_new = c_old[u] + cs[u][:, LANES:]
            c_ref[u] = c_new
            cmax = c_new if cmax is None else jnp.maximum(cmax, c_new)
        return j - 1, jnp.max(cmax)

    lax.while_loop(cond, body, (qi, jnp.float32(0.0)))

    mean_blk = _head_block_matrix(1.0 / HEAD_DIM)
    o = jnp.concatenate(
        [jnp.where(lane_lo, acc_ref[u, :Q_BLOCK], acc_ref[u, Q_BLOCK:]) for u in range(PAIRS)], axis=1)
    ms = _head_sum(o * o, mean_blk)
    o_ref[...] = _bf(o * lax.rsqrt(ms + RMS_EPS) * gain_ref[...])


def _sb(psb, gain):
    bsz, t, _ = psb.shape
    return pl.pallas_call(
        _sb_kernel,
        grid=(bsz, t // Q_BLOCK),
        in_specs=[
            pl.BlockSpec((None, Q_BLOCK, SB_WIDTH), lambda b, i: (b, i, 0)),
            pl.BlockSpec((None, t, SB_WIDTH), lambda b, i: (b, 0, 1)),
            pl.BlockSpec((None, t, SB_WIDTH), lambda b, i: (b, 0, 2)),
            pl.BlockSpec((1, SB_WIDTH), lambda b, i: (0, 0)),
        ],
        out_specs=pl.BlockSpec((None, Q_BLOCK, SB_WIDTH), lambda b, i: (b, i, 0)),
        out_shape=jax.ShapeDtypeStruct((bsz, t, SB_WIDTH), BF16),
        scratch_shapes=[
            pltpu.VMEM((PAIRS, 2 * Q_BLOCK, LANES), F32),
            pltpu.VMEM((PAIRS, 2 * Q_BLOCK, LANES), F32),
        ],
        compiler_params=pltpu.CompilerParams(
            dimension_semantics=("arbitrary", "arbitrary"), vmem_limit_bytes=VMEM_LIMIT),
        name="stickbreak",
    )(psb, psb, psb, gain)


def _out_mlp_kernel(x_ref, ya_ref, yb_ref, wo_ref, g2_ref, wu_ref, wd_ref, gf_ref, o_ref, *, final):
    x1 = (x_ref[...] + _dot(ya_ref[...], wo_ref[:RWKV_WIDTH, :])
          + _dot(yb_ref[...], wo_ref[RWKV_WIDTH:, :]))
    h = _bf(_rms(x1) * g2_ref[...])
    o_ref[...] = x1
    for c in range(D_FF // FF_CHUNK):
        cs = slice(c * FF_CHUNK, (c + 1) * FF_CHUNK)
        u = jnp.maximum(_dot(h, wu_ref[:, cs]), 0.0)
        o_ref[...] += _dot(_bf(u * u), wd_ref[cs, :])
    if final:
        o_ref[...] = _rms(o_ref[...]) * gf_ref[...]


def _out_mlp(x2d, ya, yb, wo, g2, wu, wd, gf, final):
    n = x2d.shape[0]
    const = lambda shape: pl.BlockSpec(shape, lambda i: (0, 0), pipeline_mode=pl.Buffered(1))
    return pl.pallas_call(
        functools.partial(_out_mlp_kernel, final=final),
        grid=(n // ROW_TILE,),
        in_specs=[
            pl.BlockSpec((ROW_TILE, D_MODEL), lambda i: (i, 0)),
            pl.BlockSpec((ROW_TILE, RWKV_WIDTH), lambda i: (i, 0)),
            pl.BlockSpec((ROW_TILE, SB_WIDTH), lambda i: (i, 0)),
            const((D_MODEL, D_MODEL)),
            const((1, D_MODEL)),
            const((D_MODEL, D_FF)),
            const((D_FF, D_MODEL)),
            const((1, D_MODEL)),
        ],
        out_specs=pl.BlockSpec((ROW_TILE, D_MODEL), lambda i: (i, 0)),
        out_shape=jax.ShapeDtypeStruct((n, D_MODEL), F32),
        compiler_params=pltpu.CompilerParams(
            dimension_semantics=("arbitrary",), vmem_limit_bytes=VMEM_LIMIT),
        name="outproj_mlp",
    )(x2d, ya, yb, wo, g2, wu, wd, gf)


def _lora_weight(w_decay_up, w_aaa_up, w_gate_up):
    z = lambda r: jnp.zeros((r, RWKV_WIDTH), F32)
    return _bf(jnp.concatenate([
        jnp.concatenate([w_decay_up, z(DECAY_LORA), z(DECAY_LORA)], axis=1),
        jnp.concatenate([z(AAA_LORA), w_aaa_up, z(AAA_LORA)], axis=1),
        jnp.concatenate([z(GATE_LORA), z(GATE_LORA), w_gate_up], axis=1),
    ], axis=0))


def kernel(x, ln1_g, w_in, tok_mu, w0, w_decay_up, a0, w_aaa_up, w_gate_up, k_k, k_a, r_k,
           gn_w, gn_b, sb_gain, w_out, ln2_g, w_up, w_down, lnf_g):
    bsz, t, d = x.shape
    assert d == D_MODEL and t % Q_BLOCK == 0 and (bsz * t) % ROW_TILE == 0
    depth = ln1_g.shape[0]
    row = lambda a: a.reshape(1, -1).astype(F32)
    x2d = x.reshape(bsz * t, d)
    for l in range(depth):
        pr, psb = _inproj(x2d, row(ln1_g[l]), _bf(w_in[l]))
        y_rwkv = _rwkv(pr.reshape(bsz, t, RWKV_COLS), row(tok_mu[l]), row(w0[l]), row(a0[l]),
                       row(k_k[l]), row(k_a[l]), row(r_k[l]), row(gn_w[l]), row(gn_b[l]),
                       _lora_weight(w_decay_up[l], w_aaa_up[l], w_gate_up[l]))
        y_sb = _sb(psb.reshape(bsz, t, SB_COLS), row(sb_gain[l]))
        x2d = _out_mlp(x2d, y_rwkv.reshape(bsz * t, RWKV_WIDTH), y_sb.reshape(bsz * t, SB_WIDTH),
                       _bf(w_out[l]), row(ln2_g[l]), _bf(w_up[l]), _bf(w_down[l]), row(lnf_g),
                       final=(l == depth - 1))
    return x2d.reshape(bsz, t, d)
```

```python
import functools

import jax
import jax.numpy as jnp
from jax import lax
from jax.experimental import pallas as pl
from jax.experimental.pallas import tpu as pltpu

D_MODEL = 1024
HEAD_DIM = 64
RWKV_WIDTH = 512
SB_WIDTH = 512
DECAY_LORA = 64
AAA_LORA = 64
GATE_LORA = 128
LORA_IN = DECAY_LORA + AAA_LORA + GATE_LORA
RWKV_COLS = 3 * RWKV_WIDTH + LORA_IN
SB_COLS = 3 * SB_WIDTH
IN_COLS = RWKV_COLS + SB_COLS
D_FF = 4 * D_MODEL
RMS_EPS = 1e-5
GN_EPS = 64e-5
KK_NORM_FLOOR = 1e-12
DECAY_SCALE = -0.6065306597126334

LANES = 128
PAIRS = RWKV_WIDTH // LANES
CHUNK = 64
Q_BLOCK = 128
ROW_TILE = 512
FF_CHUNK = 1024
RWKV_SEQS = 4
LOG2E = 1.4426950408889634
Q_SCALE = HEAD_DIM ** -0.5 * LOG2E
EXP2_UNDERFLOW = -151.0
SB_SEQS = 2
VMEM_LIMIT = 56 * 1024 * 1024

F32 = jnp.float32
BF16 = jnp.bfloat16


def _bf(x):
    return x.astype(BF16)


def _dot(a, b):
    return jnp.dot(a, b, preferred_element_type=F32)


def _dot_nt(a, b):
    return lax.dot_general(a, b, (((1,), (1,)), ((), ())), preferred_element_type=F32)


def _split(x):
    hi = _bf(x)
    lo = _bf(x - hi.astype(F32))
    return hi, lo


def _iota(shape, axis):
    return lax.broadcasted_iota(jnp.int32, shape, axis)


def _head_block_matrix(value):
    same = (_iota((LANES, LANES), 0) >> 6) == (_iota((LANES, LANES), 1) >> 6)
    return jnp.where(same, value, 0.0).astype(BF16)


def _head_sum(x, block):
    cols = [_dot(_bf(x[:, j * LANES:(j + 1) * LANES]), block) for j in range(x.shape[1] // LANES)]
    return cols[0] if len(cols) == 1 else jnp.concatenate(cols, axis=1)


def _sigmoid(x):
    return 1.0 / (1.0 + jnp.exp2(x * -LOG2E))


def _rms(x):
    return x * lax.rsqrt(jnp.mean(x * x, axis=-1, keepdims=True) + RMS_EPS)


def _inproj_kernel(x_ref, g_ref, w_ref, pr_ref, psb_ref):
    h = _bf(_rms(x_ref[...]) * g_ref[...])
    pr_ref[...] = _bf(_dot(h, w_ref[:, :RWKV_COLS]))
    psb = _dot(h, w_ref[:, RWKV_COLS:])
    psb_ref[:, :SB_WIDTH] = _bf(psb[:, :SB_WIDTH] * Q_SCALE)
    psb_ref[:, SB_WIDTH:] = _bf(psb[:, SB_WIDTH:])


def _inproj(x2d, g, w_bf):
    n = x2d.shape[0]
    return pl.pallas_call(
        _inproj_kernel,
        grid=(n // ROW_TILE,),
        in_specs=[
            pl.BlockSpec((ROW_TILE, D_MODEL), lambda i: (i, 0)),
            pl.BlockSpec((1, D_MODEL), lambda i: (0, 0)),
            pl.BlockSpec((D_MODEL, IN_COLS), lambda i: (0, 0)),
        ],
        out_specs=[
            pl.BlockSpec((ROW_TILE, RWKV_COLS), lambda i: (i, 0)),
            pl.BlockSpec((ROW_TILE, SB_COLS), lambda i: (i, 0)),
        ],
        out_shape=[
            jax.ShapeDtypeStruct((n, RWKV_COLS), BF16),
            jax.ShapeDtypeStruct((n, SB_COLS), BF16),
        ],
        compiler_params=pltpu.CompilerParams(
            dimension_semantics=("arbitrary",), vmem_limit_bytes=VMEM_LIMIT),
        name="inproj",
    )(x2d, g, w_bf)


def _rwkv_chains(ch, m):
    def blockdiag(y):
        return _bf(jnp.concatenate(
            [jnp.where(m["lane_lo"], y, 0.0), jnp.where(m["lane_lo"], 0.0, y)], axis=0))

    def each(fn, *lists):
        return [fn(*xs) for xs in zip(*lists)]

    at, rt, bt, kt = ([c[k] for c in ch] for k in ("at", "rt", "bt", "kt"))
    v, dl, s = ([c[k] for c in ch] for k in ("v", "dl", "s"))
    g = each(lambda a, r, b, k: _dot_nt(_bf(jnp.concatenate([a, r], axis=0)),
                                        jnp.concatenate([blockdiag(b), blockdiag(k)], axis=0)),
             at, rt, bt, kt)
    a_ab = [jnp.where(m["strict"], x[:CHUNK, :LANES], 0.0) for x in g]
    a_ak = [jnp.where(m["strict"], x[:CHUNK, LANES:], 0.0) for x in g]
    a_rb = [jnp.where(m["incl"], x[CHUNK:, :LANES], 0.0) for x in g]
    a_rk = [jnp.where(m["incl"], x[CHUNK:, LANES:], 0.0) for x in g]

    t = [m["eye"] + a for a in a_ab]
    x = each(lambda a: _dot(_bf(a), blockdiag(a)), a_ab)
    for _ in range(4):
        tx = each(lambda ti, xi: _dot(_bf(jnp.concatenate([ti, xi], axis=0)), blockdiag(xi)), t, x)
        t = each(lambda ti, txi: ti + txi[:CHUNK], t, tx)
        x = [txi[CHUNK:] for txi in tx]
    t = each(lambda ti, xi: ti + _dot(_bf(ti), blockdiag(xi)), t, x)

    v_bd = [blockdiag(vi) for vi in v]
    av = each(lambda a, vb: _dot(_bf(a), vb), a_ak, v_bd)
    wu = each(lambda ti, a, avi: _dot(_bf(ti), jnp.concatenate([blockdiag(a), blockdiag(avi)], axis=1)),
              t, at, av)
    w = [x_[:, :LANES] for x_ in wu]
    u0 = [x_[:, LANES:] for x_ in wu]
    ry = each(lambda a, wi, ui: _dot(_bf(a), jnp.concatenate([blockdiag(wi), blockdiag(ui)], axis=1)),
              a_rb, w, u0)
    yk = each(lambda a, vb: _dot(_bf(a), vb), a_rk, v_bd)
    rhat = each(lambda r, x_: r + x_[:, :LANES], rt, ry)
    y0 = each(lambda x_, k_: x_[:, LANES:] + k_, ry, yk)

    def transition(c, wi, ui, vi):
        bk_t = jnp.concatenate([c["bd"], c["kd"]], axis=0).T
        rhs = jnp.concatenate(
            [jnp.concatenate([wi, ui], axis=1),
             jnp.concatenate([jnp.zeros_like(vi), vi], axis=1)], axis=0)
        return _dot(_bf(bk_t), _bf(rhs))

    mn = each(transition, ch, w, u0, v)
    m_bd = each(lambda x_, d: jnp.where(m["same_head"], x_[:, :LANES], 0.0) + jnp.where(m["diag"], d, 0.0),
                mn, dl)
    n_bd = [jnp.where(m["same_head"], x_[:, LANES:], 0.0) for x_ in mn]
    rs = each(lambda r, mb, si: _dot(_bf(jnp.concatenate([r, mb], axis=0)), _bf(si)), rhat, m_bd, s)
    y = each(lambda x_, y_: x_[:CHUNK] + y_, rs, y0)
    s_new = each(lambda x_, n_: x_[CHUNK:] + n_, rs, n_bd)
    return y, s_new


def _rwkv_kernel(p_ref, mu_ref, w0_ref, a0_ref, kk_ref, ka_ref, rk_ref, gw_ref, gb_ref, wl_ref,
                 y_ref, carry_ref, state_ref, *, seqs):
    @pl.when(pl.program_id(1) == 0)
    def _():
        carry_ref[...] = jnp.zeros_like(carry_ref)
        state_ref[...] = jnp.zeros_like(state_ref)

    rows = seqs * CHUNK
    w = RWKV_WIDTH
    first_row = _iota((8, RWKV_COLS), 0) == 0

    p_parts, prev_parts = [], []
    for b in range(seqs):
        pb = p_ref[b].astype(F32)
        rolled = pltpu.roll(pb, 1, 0)
        head = jnp.where(first_row, carry_ref[b], rolled[:8])
        prev_parts.append(jnp.concatenate([head, rolled[8:]], axis=0))
        carry_ref[b] = pb[CHUNK - 1:CHUNK]
        p_parts.append(pb)
    p = jnp.concatenate(p_parts, axis=0)
    prev = jnp.concatenate(prev_parts, axis=0)
    p = p + mu_ref[...] * (prev - p)

    r = p[:, :w]
    k = p[:, w:2 * w]
    v = p[:, 2 * w:3 * w]
    xl = p[:, 3 * w:]
    lane_l = _iota((rows, LORA_IN), 1)
    lora_in = jnp.where(lane_l < DECAY_LORA, jnp.tanh(xl),
                        jnp.where(lane_l < DECAY_LORA + AAA_LORA, xl, _sigmoid(xl)))
    lora = _dot(_bf(lora_in), wl_ref[...])
    logd = DECAY_SCALE * _sigmoid(w0_ref[...] + lora[:, :w])
    asig = _sigmoid(a0_ref[...] + lora[:, w:2 * w])
    gate = lora[:, 2 * w:]

    ones_blk = _head_block_matrix(1.0)
    mean_blk = _head_block_matrix(1.0 / HEAD_DIM)
    kkr = k * kk_ref[...]
    kk = kkr * jnp.minimum(lax.rsqrt(_head_sum(kkr * kkr, ones_blk)), 1.0 / KK_NORM_FLOOR)
    kmod = k * (1.0 + (asig - 1.0) * ka_ref[...])
    a_vec = -kk
    b_vec = kk * asig
    bonus = _head_sum(r * kmod * rk_ref[...], ones_blk) * v

    row_t = _iota((CHUNK, LANES), 0)
    col_i = _iota((CHUNK, LANES), 1) & (HEAD_DIM - 1)
    row_l = _iota((LANES, LANES), 0)
    col_l = _iota((LANES, LANES), 1)
    masks = {
        "lane_lo": _iota((CHUNK, LANES), 1) < HEAD_DIM,
        "strict": row_t > col_i,
        "incl": row_t >= col_i,
        "eye": jnp.where(row_t == col_i, 1.0, 0.0).astype(F32),
        "same_head": (row_l >> 6) == (col_l >> 6),
        "diag": row_l == col_l,
    }
    tri = jnp.where(_iota((CHUNK, CHUNK), 0) >= _iota((CHUNK, CHUNK), 1), 1.0, 0.0).astype(BF16)

    chains = []
    for b in range(seqs):
        sl = slice(b * CHUNK, (b + 1) * CHUNK)
        ld = logd[sl]
        hi, lo = _split(ld)
        cl = _dot(tri, hi) + _dot(tri, lo)
        e_pos = jnp.exp(cl)
        e_neg = jnp.exp(-cl)
        e_prev = jnp.exp(cl - ld)
        dl = e_pos[CHUNK - 1:CHUNK]
        at = a_vec[sl] * e_prev
        rt = r[sl] * e_pos
        bt = b_vec[sl] * e_neg
        kt = kmod[sl] * e_neg
        bd_ = bt * dl
        kd_ = kt * dl
        vb = v[sl]
        for pp in range(PAIRS):
            ls = slice(pp * LANES, (pp + 1) * LANES)
            chains.append(dict(at=at[:, ls], rt=rt[:, ls], bt=bt[:, ls], kt=kt[:, ls], bd=bd_[:, ls],
                               kd=kd_[:, ls], v=vb[:, ls], dl=dl[:, ls], s=state_ref[b, pp]))
    y_chains, s_chains = _rwkv_chains(chains, masks)
    for b in range(seqs):
        for pp in range(PAIRS):
            state_ref[b, pp] = s_chains[b * PAIRS + pp]
    y = jnp.concatenate(
        [jnp.concatenate(y_chains[b * PAIRS:(b + 1) * PAIRS], axis=1) for b in range(seqs)], axis=0)

    mean = _head_sum(y, mean_blk)
    yc = y - mean
    var = _head_sum(yc * yc, mean_blk)
    yn = yc * lax.rsqrt(var + GN_EPS) * gw_ref[...] + gb_ref[...]
    out = _bf((yn + bonus) * gate)
    for b in range(seqs):
        y_ref[b] = out[b * CHUNK:(b + 1) * CHUNK]


def _rwkv(pr, mu, w0, a0, k_k, k_a, r_k, gn_w, gn_b, w_lora):
    bsz, t, _ = pr.shape
    seqs = RWKV_SEQS if bsz % RWKV_SEQS == 0 else 1
    vec = lambda c: pl.BlockSpec((1, c), lambda b, j: (0, 0))
    return pl.pallas_call(
        functools.partial(_rwkv_kernel, seqs=seqs),
        grid=(bsz // seqs, t // CHUNK),
        in_specs=[
            pl.BlockSpec((seqs, CHUNK, RWKV_COLS), lambda b, j: (b, j, 0)),
            vec(RWKV_COLS), vec(RWKV_WIDTH), vec(RWKV_WIDTH), vec(RWKV_WIDTH), vec(RWKV_WIDTH),
            vec(RWKV_WIDTH), vec(RWKV_WIDTH), vec(RWKV_WIDTH),
            pl.BlockSpec((LORA_IN, 3 * RWKV_WIDTH), lambda b, j: (0, 0)),
        ],
        out_specs=pl.BlockSpec((seqs, CHUNK, RWKV_WIDTH), lambda b, j: (b, j, 0)),
        out_shape=jax.ShapeDtypeStruct((bsz, t, RWKV_WIDTH), BF16),
        scratch_shapes=[
            pltpu.VMEM((seqs, 1, RWKV_COLS), F32),
            pltpu.VMEM((seqs, PAIRS, LANES, LANES), F32),
        ],
        compiler_params=pltpu.CompilerParams(
            dimension_semantics=("arbitrary", "arbitrary"), vmem_limit_bytes=VMEM_LIMIT),
        name="rwkv7",
    )(pr, mu, w0, a0, k_k, k_a, r_k, gn_w, gn_b, w_lora)


def _sb_kernel(q_ref, k_ref, v_ref, gain_ref, o_ref, c_ref, acc_ref, *, seqs):
    qi = pl.program_id(1)
    rows2 = 2 * Q_BLOCK
    units = [(s, u) for s in range(seqs) for u in range(PAIRS)]
    lanes = lambda u: slice(u * LANES, (u + 1) * LANES)
    lane_lo = _iota((Q_BLOCK, LANES), 1) < HEAD_DIM
    q2 = []
    for s, u in units:
        q = q_ref[s, :, lanes(u)]
        zero = jnp.zeros_like(q)
        q2.append(jnp.concatenate([jnp.where(lane_lo, q, zero), jnp.where(lane_lo, zero, q)], axis=0))

    causal = _iota((rows2, LANES), 1) < (_iota((rows2, LANES), 0) & (Q_BLOCK - 1))
    neg_cum = jnp.where(
        (_iota((LANES, 2 * LANES), 0) > _iota((LANES, 2 * LANES), 1))
        | (_iota((LANES, 2 * LANES), 1) >= LANES), -1.0, 0.0).astype(BF16)

    def softplus2(x):
        return jnp.maximum(x, 0.0) + jnp.log2(1.0 + jnp.exp2(-jnp.abs(x)))

    def tile(j, diagonal):
        start = pl.multiple_of(j * Q_BLOCK, Q_BLOCK)
        kj = [k_ref[s, pl.ds(start, Q_BLOCK), :] for s in range(seqs)]
        vj = [v_ref[s, pl.ds(start, Q_BLOCK), :] for s in range(seqs)]
        z = [_dot_nt(q2[i], kj[s][:, lanes(u)]) for i, (s, u) in enumerate(units)]
        sp = [softplus2(x) for x in z]
        cs = [_dot(_bf(jnp.where(causal, x, 0.0) if diagonal else x), neg_cum) for x in sp]
        cmax = None
        for i, (s, u) in enumerate(units):
            log_a = z[i] - sp[i] + cs[i][:, :LANES]
            if not diagonal:
                log_a = log_a + c_ref[i]
            att = jnp.exp2(log_a)
            if diagonal:
                att = jnp.where(causal, att, 0.0)
            pv = _dot(_bf(att), vj[s][:, lanes(u)])
            if diagonal:
                acc_ref[i] = pv
                c_new = cs[i][:, LANES:]
            else:
                acc_ref[i] += pv
                c_new = c_ref[i] + cs[i][:, LANES:]
            c_ref[i] = c_new
            cmax = c_new if cmax is None else jnp.maximum(cmax, c_new)
        return jnp.max(cmax)

    def cond(carry):
        j, cmax = carry
        return jnp.logical_and(j >= 0, cmax >= EXP2_UNDERFLOW)

    def body(carry):
        j, _ = carry
        return j - 1, tile(j, False)

    lax.while_loop(cond, body, (qi - 1, tile(qi, True)))

    mean_blk = _head_block_matrix(1.0 / HEAD_DIM)
    for s in range(seqs):
        o = jnp.concatenate(
            [jnp.where(lane_lo, acc_ref[s * PAIRS + u, :Q_BLOCK], acc_ref[s * PAIRS + u, Q_BLOCK:])
             for u in range(PAIRS)], axis=1)
        ms = _head_sum(o * o, mean_blk)
        o_ref[s] = _bf(o * lax.rsqrt(ms + RMS_EPS) * gain_ref[...])


def _sb(psb, gain):
    bsz, t, _ = psb.shape
    seqs = SB_SEQS if bsz % SB_SEQS == 0 else 1
    return pl.pallas_call(
        functools.partial(_sb_kernel, seqs=seqs),
        grid=(bsz // seqs, t // Q_BLOCK),
        in_specs=[
            pl.BlockSpec((seqs, Q_BLOCK, SB_WIDTH), lambda b, i: (b, i, 0)),
            pl.BlockSpec((seqs, t, SB_WIDTH), lambda b, i: (b, 0, 1)),
            pl.BlockSpec((seqs, t, SB_WIDTH), lambda b, i: (b, 0, 2)),
            pl.BlockSpec((1, SB_WIDTH), lambda b, i: (0, 0)),
        ],
        out_specs=pl.BlockSpec((seqs, Q_BLOCK, SB_WIDTH), lambda b, i: (b, i, 0)),
        out_shape=jax.ShapeDtypeStruct((bsz, t, SB_WIDTH), BF16),
        scratch_shapes=[
            pltpu.VMEM((seqs * PAIRS, 2 * Q_BLOCK, LANES), F32),
            pltpu.VMEM((seqs * PAIRS, 2 * Q_BLOCK, LANES), F32),
        ],
        compiler_params=pltpu.CompilerParams(
            dimension_semantics=("arbitrary", "arbitrary"), vmem_limit_bytes=VMEM_LIMIT),
        name="stickbreak",
    )(psb, psb, psb, gain)


def _out_mlp_kernel(x_ref, ya_ref, yb_ref, wo_ref, g2_ref, wu_ref, wd_ref, gf_ref, o_ref, *, final):
    x1 = (x_ref[...] + _dot(ya_ref[...], wo_ref[:RWKV_WIDTH, :])
          + _dot(yb_ref[...], wo_ref[RWKV_WIDTH:, :]))
    h = _bf(_rms(x1) * g2_ref[...])
    o_ref[...] = x1
    for c in range(D_FF // FF_CHUNK):
        cs = slice(c * FF_CHUNK, (c + 1) * FF_CHUNK)
        u = jnp.maximum(_dot(h, wu_ref[:, cs]), 0.0)
        o_ref[...] += _dot(_bf(u * u), wd_ref[cs, :])
    if final:
        o_ref[...] = _rms(o_ref[...]) * gf_ref[...]


def _out_mlp(x2d, ya, yb, wo, g2, wu, wd, gf, final):
    n = x2d.shape[0]
    const = lambda shape: pl.BlockSpec(shape, lambda i: (0, 0), pipeline_mode=pl.Buffered(1))
    return pl.pallas_call(
        functools.partial(_out_mlp_kernel, final=final),
        grid=(n // ROW_TILE,),
        in_specs=[
            pl.BlockSpec((ROW_TILE, D_MODEL), lambda i: (i, 0)),
            pl.BlockSpec((ROW_TILE, RWKV_WIDTH), lambda i: (i, 0)),
            pl.BlockSpec((ROW_TILE, SB_WIDTH), lambda i: (i, 0)),
            const((D_MODEL, D_MODEL)),
            const((1, D_MODEL)),
            const((D_MODEL, D_FF)),
            const((D_FF, D_MODEL)),
            const((1, D_MODEL)),
        ],
        out_specs=pl.BlockSpec((ROW_TILE, D_MODEL), lambda i: (i, 0)),
        out_shape=jax.ShapeDtypeStruct((n, D_MODEL), F32),
        compiler_params=pltpu.CompilerParams(
            dimension_semantics=("arbitrary",), vmem_limit_bytes=VMEM_LIMIT),
        name="outproj_mlp",
    )(x2d, ya, yb, wo, g2, wu, wd, gf)


def _lora_weight(w_decay_up, w_aaa_up, w_gate_up):
    z = lambda r: jnp.zeros((r, RWKV_WIDTH), F32)
    return _bf(jnp.concatenate([
        jnp.concatenate([w_decay_up, z(DECAY_LORA), z(DECAY_LORA)], axis=1),
        jnp.concatenate([z(AAA_LORA), w_aaa_up, z(AAA_LORA)], axis=1),
        jnp.concatenate([z(GATE_LORA), z(GATE_LORA), w_gate_up], axis=1),
    ], axis=0))


def kernel(x, ln1_g, w_in, tok_mu, w0, w_decay_up, a0, w_aaa_up, w_gate_up, k_k, k_a, r_k,
           gn_w, gn_b, sb_gain, w_out, ln2_g, w_up, w_down, lnf_g):
    bsz, t, d = x.shape
    assert d == D_MODEL and t % Q_BLOCK == 0 and (bsz * t) % ROW_TILE == 0
    depth = ln1_g.shape[0]
    row = lambda a: a.reshape(1, -1).astype(F32)
    x2d = x.reshape(bsz * t, d)
    for l in range(depth):
        pr, psb = _inproj(x2d, row(ln1_g[l]), _bf(w_in[l]))
        y_rwkv = _rwkv(pr.reshape(bsz, t, RWKV_COLS), row(tok_mu[l]), row(w0[l]), row(a0[l]),
                       row(k_k[l]), row(k_a[l]), row(r_k[l]), row(gn_w[l]), row(gn_b[l]),
                       _lora_weight(w_decay_up[l], w_aaa_up[l], w_gate_up[l]))
        y_sb = _sb(psb.reshape(bsz, t, SB_COLS), row(sb_gain[l]))
        x2d = _out_mlp(x2d, y_rwkv.reshape(bsz * t, RWKV_WIDTH), y_sb.reshape(bsz * t, SB_WIDTH),
                       _bf(w_out[l]), row(ln2_g[l]), _bf(w_up[l]), _bf(w_down[l]), row(lnf_g),
                       final=(l == depth - 1))
    return x2d.reshape(bsz, t, d)
```

```python
import functools

import jax
import jax.numpy as jnp
from jax import lax
from jax.experimental import pallas as pl
from jax.experimental.pallas import tpu as pltpu

D_MODEL = 1024
HEAD_DIM = 64
RWKV_WIDTH = 512
SB_WIDTH = 512
DECAY_LORA = 64
AAA_LORA = 64
GATE_LORA = 128
LORA_IN = DECAY_LORA + AAA_LORA + GATE_LORA
RWKV_COLS = 3 * RWKV_WIDTH + LORA_IN
SB_COLS = 3 * SB_WIDTH
IN_COLS = RWKV_COLS + SB_COLS
D_FF = 4 * D_MODEL
RMS_EPS = 1e-5
GN_EPS = 64e-5
KK_NORM_FLOOR = 1e-12
DECAY_SCALE = -0.6065306597126334

LANES = 128
PAIRS = RWKV_WIDTH // LANES
CHUNK = 64
Q_BLOCK = 128
ROW_TILE = 512
FF_CHUNK = 1024
RWKV_SEQS = 4
LOG2E = 1.4426950408889634
Q_SCALE = HEAD_DIM ** -0.5 * LOG2E
EXP2_UNDERFLOW = -151.0
SB_SEQS = 4
VMEM_LIMIT = 56 * 1024 * 1024

F32 = jnp.float32
BF16 = jnp.bfloat16


def _bf(x):
    return x.astype(BF16)


def _dot(a, b):
    return jnp.dot(a, b, preferred_element_type=F32)


def _dot_nt(a, b):
    return lax.dot_general(a, b, (((1,), (1,)), ((), ())), preferred_element_type=F32)


def _split(x):
    hi = _bf(x)
    lo = _bf(x - hi.astype(F32))
    return hi, lo


def _iota(shape, axis):
    return lax.broadcasted_iota(jnp.int32, shape, axis)


def _head_block_matrix(value):
    same = (_iota((LANES, LANES), 0) >> 6) == (_iota((LANES, LANES), 1) >> 6)
    return jnp.where(same, value, 0.0).astype(BF16)


def _head_sum(x, block):
    cols = [_dot(_bf(x[:, j * LANES:(j + 1) * LANES]), block) for j in range(x.shape[1] // LANES)]
    return cols[0] if len(cols) == 1 else jnp.concatenate(cols, axis=1)


def _sigmoid(x):
    return 1.0 / (1.0 + jnp.exp2(x * -LOG2E))


def _rms(x):
    return x * lax.rsqrt(jnp.mean(x * x, axis=-1, keepdims=True) + RMS_EPS)


def _inproj_kernel(x_ref, g_ref, w_ref, pr_ref, psb_ref):
    h = _bf(_rms(x_ref[...]) * g_ref[...])
    pr_ref[...] = _bf(_dot(h, w_ref[:, :RWKV_COLS]))
    psb = _dot(h, w_ref[:, RWKV_COLS:])
    psb_ref[:, :SB_WIDTH] = _bf(psb[:, :SB_WIDTH] * Q_SCALE)
    psb_ref[:, SB_WIDTH:] = _bf(psb[:, SB_WIDTH:])


def _inproj(x2d, g, w_bf):
    n = x2d.shape[0]
    return pl.pallas_call(
        _inproj_kernel,
        grid=(n // ROW_TILE,),
        in_specs=[
            pl.BlockSpec((ROW_TILE, D_MODEL), lambda i: (i, 0)),
            pl.BlockSpec((1, D_MODEL), lambda i: (0, 0)),
            pl.BlockSpec((D_MODEL, IN_COLS), lambda i: (0, 0)),
        ],
        out_specs=[
            pl.BlockSpec((ROW_TILE, RWKV_COLS), lambda i: (i, 0)),
            pl.BlockSpec((ROW_TILE, SB_COLS), lambda i: (i, 0)),
        ],
        out_shape=[
            jax.ShapeDtypeStruct((n, RWKV_COLS), BF16),
            jax.ShapeDtypeStruct((n, SB_COLS), BF16),
        ],
        compiler_params=pltpu.CompilerParams(
            dimension_semantics=("arbitrary",), vmem_limit_bytes=VMEM_LIMIT),
        name="inproj",
    )(x2d, g, w_bf)


def _rwkv_chains(ch, m):
    def blockdiag(y):
        return _bf(jnp.concatenate(
            [jnp.where(m["lane_lo"], y, 0.0), jnp.where(m["lane_lo"], 0.0, y)], axis=0))

    def each(fn, *lists):
        return [fn(*xs) for xs in zip(*lists)]

    at, rt, bt, kt = ([c[k] for c in ch] for k in ("at", "rt", "bt", "kt"))
    v, dl, s = ([c[k] for c in ch] for k in ("v", "dl", "s"))
    g = each(lambda a, r, b, k: _dot_nt(_bf(jnp.concatenate([a, r], axis=0)),
                                        jnp.concatenate([blockdiag(b), blockdiag(k)], axis=0)),
             at, rt, bt, kt)
    a_ab = [jnp.where(m["strict"], x[:CHUNK, :LANES], 0.0) for x in g]
    a_ak = [jnp.where(m["strict"], x[:CHUNK, LANES:], 0.0) for x in g]
    a_rb = [jnp.where(m["incl"], x[CHUNK:, :LANES], 0.0) for x in g]
    a_rk = [jnp.where(m["incl"], x[CHUNK:, LANES:], 0.0) for x in g]

    t = [m["eye"] + a for a in a_ab]
    x = each(lambda a: _dot(_bf(a), blockdiag(a)), a_ab)
    for _ in range(4):
        tx = each(lambda ti, xi: _dot(_bf(jnp.concatenate([ti, xi], axis=0)), blockdiag(xi)), t, x)
        t = each(lambda ti, txi: ti + txi[:CHUNK], t, tx)
        x = [txi[CHUNK:] for txi in tx]
    t = each(lambda ti, xi: ti + _dot(_bf(ti), blockdiag(xi)), t, x)

    v_bd = [blockdiag(vi) for vi in v]
    av = each(lambda a, vb: _dot(_bf(a), vb), a_ak, v_bd)
    wu = each(lambda ti, a, avi: _dot(_bf(ti), jnp.concatenate([blockdiag(a), blockdiag(avi)], axis=1)),
              t, at, av)
    w = [x_[:, :LANES] for x_ in wu]
    u0 = [x_[:, LANES:] for x_ in wu]
    ry = each(lambda a, wi, ui: _dot(_bf(a), jnp.concatenate([blockdiag(wi), blockdiag(ui)], axis=1)),
              a_rb, w, u0)
    yk = each(lambda a, vb: _dot(_bf(a), vb), a_rk, v_bd)
    rhat = each(lambda r, x_: r + x_[:, :LANES], rt, ry)
    y0 = each(lambda x_, k_: x_[:, LANES:] + k_, ry, yk)

    def transition(c, wi, ui, vi):
        bk_t = jnp.concatenate([c["bd"], c["kd"]], axis=0).T
        rhs = jnp.concatenate(
            [jnp.concatenate([wi, ui], axis=1),
             jnp.concatenate([jnp.zeros_like(vi), vi], axis=1)], axis=0)
        return _dot(_bf(bk_t), _bf(rhs))

    mn = each(transition, ch, w, u0, v)
    m_bd = each(lambda x_, d: jnp.where(m["same_head"], x_[:, :LANES], 0.0) + jnp.where(m["diag"], d, 0.0),
                mn, dl)
    n_bd = [jnp.where(m["same_head"], x_[:, LANES:], 0.0) for x_ in mn]
    rs = each(lambda r, mb, si: _dot(_bf(jnp.concatenate([r, mb], axis=0)), _bf(si)), rhat, m_bd, s)
    y = each(lambda x_, y_: x_[:CHUNK] + y_, rs, y0)
    s_new = each(lambda x_, n_: x_[CHUNK:] + n_, rs, n_bd)
    return y, s_new


def _rwkv_kernel(p_ref, mu_ref, w0_ref, a0_ref, kk_ref, ka_ref, rk_ref, gw_ref, gb_ref, wl_ref,
                 y_ref, carry_ref, state_ref, *, seqs):
    @pl.when(pl.program_id(1) == 0)
    def _():
        carry_ref[...] = jnp.zeros_like(carry_ref)
        state_ref[...] = jnp.zeros_like(state_ref)

    rows = seqs * CHUNK
    w = RWKV_WIDTH
    first_row = _iota((8, RWKV_COLS), 0) == 0

    p_parts, prev_parts = [], []
    for b in range(seqs):
        pb = p_ref[b].astype(F32)
        rolled = pltpu.roll(pb, 1, 0)
        head = jnp.where(first_row, carry_ref[b], rolled[:8])
        prev_parts.append(jnp.concatenate([head, rolled[8:]], axis=0))
        carry_ref[b] = pb[CHUNK - 1:CHUNK]
        p_parts.append(pb)
    p = jnp.concatenate(p_parts, axis=0)
    prev = jnp.concatenate(prev_parts, axis=0)
    p = p + mu_ref[...] * (prev - p)

    r = p[:, :w]
    k = p[:, w:2 * w]
    v = p[:, 2 * w:3 * w]
    xl = p[:, 3 * w:]
    lane_l = _iota((rows, LORA_IN), 1)
    lora_in = jnp.where(lane_l < DECAY_LORA, jnp.tanh(xl),
                        jnp.where(lane_l < DECAY_LORA + AAA_LORA, xl, _sigmoid(xl)))
    lora = _dot(_bf(lora_in), wl_ref[...])
    logd = DECAY_SCALE * _sigmoid(w0_ref[...] + lora[:, :w])
    asig = _sigmoid(a0_ref[...] + lora[:, w:2 * w])
    gate = lora[:, 2 * w:]

    ones_blk = _head_block_matrix(1.0)
    mean_blk = _head_block_matrix(1.0 / HEAD_DIM)
    kkr = k * kk_ref[...]
    kk = kkr * jnp.minimum(lax.rsqrt(_head_sum(kkr * kkr, ones_blk)), 1.0 / KK_NORM_FLOOR)
    kmod = k * (1.0 + (asig - 1.0) * ka_ref[...])
    a_vec = -kk
    b_vec = kk * asig
    bonus = _head_sum(r * kmod * rk_ref[...], ones_blk) * v

    row_t = _iota((CHUNK, LANES), 0)
    col_i = _iota((CHUNK, LANES), 1) & (HEAD_DIM - 1)
    row_l = _iota((LANES, LANES), 0)
    col_l = _iota((LANES, LANES), 1)
    masks = {
        "lane_lo": _iota((CHUNK, LANES), 1) < HEAD_DIM,
        "strict": row_t > col_i,
        "incl": row_t >= col_i,
        "eye": jnp.where(row_t == col_i, 1.0, 0.0).astype(F32),
        "same_head": (row_l >> 6) == (col_l >> 6),
        "diag": row_l == col_l,
    }
    tri = jnp.where(_iota((CHUNK, CHUNK), 0) >= _iota((CHUNK, CHUNK), 1), 1.0, 0.0).astype(BF16)

    chains = []
    for b in range(seqs):
        sl = slice(b * CHUNK, (b + 1) * CHUNK)
        ld = logd[sl]
        hi, lo = _split(ld)
        cl = _dot(tri, hi) + _dot(tri, lo)
        e_pos = jnp.exp(cl)
        e_neg = jnp.exp(-cl)
        e_prev = jnp.exp(cl - ld)
        dl = e_pos[CHUNK - 1:CHUNK]
        at = a_vec[sl] * e_prev
        rt = r[sl] * e_pos
        bt = b_vec[sl] * e_neg
        kt = kmod[sl] * e_neg
        bd_ = bt * dl
        kd_ = kt * dl
        vb = v[sl]
        for pp in range(PAIRS):
            ls = slice(pp * LANES, (pp + 1) * LANES)
            chains.append(dict(at=at[:, ls], rt=rt[:, ls], bt=bt[:, ls], kt=kt[:, ls], bd=bd_[:, ls],
                               kd=kd_[:, ls], v=vb[:, ls], dl=dl[:, ls], s=state_ref[b, pp]))
    y_chains, s_chains = _rwkv_chains(chains, masks)
    for b in range(seqs):
        for pp in range(PAIRS):
            state_ref[b, pp] = s_chains[b * PAIRS + pp]
    y = jnp.concatenate(
        [jnp.concatenate(y_chains[b * PAIRS:(b + 1) * PAIRS], axis=1) for b in range(seqs)], axis=0)

    mean = _head_sum(y, mean_blk)
    yc = y - mean
    var = _head_sum(yc * yc, mean_blk)
    yn = yc * lax.rsqrt(var + GN_EPS) * gw_ref[...] + gb_ref[...]
    out = _bf((yn + bonus) * gate)
    for b in range(seqs):
        y_ref[b] = out[b * CHUNK:(b + 1) * CHUNK]


def _rwkv(pr, mu, w0, a0, k_k, k_a, r_k, gn_w, gn_b, w_lora):
    bsz, t, _ = pr.shape
    seqs = RWKV_SEQS if bsz % RWKV_SEQS == 0 else 1
    vec = lambda c: pl.BlockSpec((1, c), lambda b, j: (0, 0))
    return pl.pallas_call(
        functools.partial(_rwkv_kernel, seqs=seqs),
        grid=(bsz // seqs, t // CHUNK),
        in_specs=[
            pl.BlockSpec((seqs, CHUNK, RWKV_COLS), lambda b, j: (b, j, 0)),
            vec(RWKV_COLS), vec(RWKV_WIDTH), vec(RWKV_WIDTH), vec(RWKV_WIDTH), vec(RWKV_WIDTH),
            vec(RWKV_WIDTH), vec(RWKV_WIDTH), vec(RWKV_WIDTH),
            pl.BlockSpec((LORA_IN, 3 * RWKV_WIDTH), lambda b, j: (0, 0)),
        ],
        out_specs=pl.BlockSpec((seqs, CHUNK, RWKV_WIDTH), lambda b, j: (b, j, 0)),
        out_shape=jax.ShapeDtypeStruct((bsz, t, RWKV_WIDTH), BF16),
        scratch_shapes=[
            pltpu.VMEM((seqs, 1, RWKV_COLS), F32),
            pltpu.VMEM((seqs, PAIRS, LANES, LANES), F32),
        ],
        compiler_params=pltpu.CompilerParams(
            dimension_semantics=("arbitrary", "arbitrary"), vmem_limit_bytes=VMEM_LIMIT),
        name="rwkv7",
    )(pr, mu, w0, a0, k_k, k_a, r_k, gn_w, gn_b, w_lora)


def _sb_kernel(q_ref, k_ref, v_ref, gain_ref, o_ref, c_ref, acc_ref, *, seqs):
    qi = pl.program_id(1)
    half = Q_BLOCK // 2
    units = [(s, u, h) for s in range(seqs) for u in range(PAIRS) for h in range(2)]
    lanes = lambda u: slice(u * LANES, (u + 1) * LANES)
    lane_lo = _iota((half, LANES), 1) < HEAD_DIM
    q2 = []
    for s, u, h in units:
        q = q_ref[s, h * half:(h + 1) * half, lanes(u)]
        zero = jnp.zeros_like(q)
        q2.append(jnp.concatenate([jnp.where(lane_lo, q, zero), jnp.where(lane_lo, zero, q)], axis=0))

    row_q = _iota((Q_BLOCK, LANES), 0) & (half - 1)
    col_k = _iota((Q_BLOCK, LANES), 1)
    neg_cum = jnp.where(
        (_iota((LANES, 2 * LANES), 0) > _iota((LANES, 2 * LANES), 1))
        | (_iota((LANES, 2 * LANES), 1) >= LANES), -1.0, 0.0).astype(BF16)
    q_start = qi * Q_BLOCK

    def softplus2(x):
        return jnp.maximum(x, 0.0) + jnp.log2(1.0 + jnp.exp2(-jnp.abs(x)))

    def tile(starts, visible, first):
        kj = [[k_ref[s, pl.ds(starts[h], Q_BLOCK), :] for h in range(2)] for s in range(seqs)]
        vj = [[v_ref[s, pl.ds(starts[h], Q_BLOCK), :] for h in range(2)] for s in range(seqs)]
        z = [_dot_nt(q2[i], kj[s][h][:, lanes(u)]) for i, (s, u, h) in enumerate(units)]
        sp = [softplus2(x) for x in z]
        cs = [_dot(_bf(x if visible[h] is None else jnp.where(visible[h], x, 0.0)), neg_cum)
              for x, (s, u, h) in zip(sp, units)]
        cmax = None
        for i, (s, u, h) in enumerate(units):
            log_a = z[i] - sp[i] + cs[i][:, :LANES]
            if not first:
                log_a = log_a + c_ref[i]
            att = jnp.exp2(log_a)
            if visible[h] is not None:
                att = jnp.where(visible[h], att, 0.0)
            pv = _dot(_bf(att), vj[s][h][:, lanes(u)])
            if first:
                acc_ref[i] = pv
                c_new = cs[i][:, LANES:]
            else:
                acc_ref[i] += pv
                c_new = c_ref[i] + cs[i][:, LANES:]
            c_ref[i] = c_new
            cmax = c_new if cmax is None else jnp.maximum(cmax, c_new)
        return jnp.max(cmax)

    start0 = [jnp.maximum(q_start + (h - 1) * half, 0) for h in range(2)]
    first_visible = [col_k < row_q + (q_start + h * half - start0[h]) for h in range(2)]
    cmax0 = tile([pl.multiple_of(x, half) for x in start0], first_visible, True)

    def cond(carry):
        i, cmax = carry
        return jnp.logical_and(i < qi, cmax >= EXP2_UNDERFLOW)

    def body(carry):
        i, _ = carry
        starts = [pl.multiple_of(q_start + (h - 1) * half - i * Q_BLOCK, half) for h in range(2)]
        return i + 1, tile(starts, [None, None], False)

    i_end, cmax_end = lax.while_loop(cond, body, (jnp.int32(1), cmax0))

    @pl.when(jnp.logical_and(jnp.logical_and(qi >= 1, i_end == qi), cmax_end >= EXP2_UNDERFLOW))
    def _():
        tile([0, 0], [col_k < half, None], False)

    mean_blk = _head_block_matrix(1.0 / HEAD_DIM)
    for s in range(seqs):
        rows = []
        for h in range(2):
            acc = [acc_ref[(s * PAIRS + u) * 2 + h] for u in range(PAIRS)]
            rows.append(jnp.concatenate([jnp.where(lane_lo, a[:half], a[half:]) for a in acc], axis=1))
        o = jnp.concatenate(rows, axis=0)
        ms = _head_sum(o * o, mean_blk)
        o_ref[s] = _bf(o * lax.rsqrt(ms + RMS_EPS) * gain_ref[...])


def _sb(psb, gain):
    bsz, t, _ = psb.shape
    seqs = SB_SEQS if bsz % SB_SEQS == 0 else 1
    return pl.pallas_call(
        functools.partial(_sb_kernel, seqs=seqs),
        grid=(bsz // seqs, t // Q_BLOCK),
        in_specs=[
            pl.BlockSpec((seqs, Q_BLOCK, SB_WIDTH), lambda b, i: (b, i, 0)),
            pl.BlockSpec((seqs, t, SB_WIDTH), lambda b, i: (b, 0, 1)),
            pl.BlockSpec((seqs, t, SB_WIDTH), lambda b, i: (b, 0, 2)),
            pl.BlockSpec((1, SB_WIDTH), lambda b, i: (0, 0)),
        ],
        out_specs=pl.BlockSpec((seqs, Q_BLOCK, SB_WIDTH), lambda b, i: (b, i, 0)),
        out_shape=jax.ShapeDtypeStruct((bsz, t, SB_WIDTH), BF16),
        scratch_shapes=[
            pltpu.VMEM((2 * seqs * PAIRS, Q_BLOCK, LANES), F32),
            pltpu.VMEM((2 * seqs * PAIRS, Q_BLOCK, LANES), F32),
        ],
        compiler_params=pltpu.CompilerParams(
            dimension_semantics=("arbitrary", "arbitrary"), vmem_limit_bytes=VMEM_LIMIT),
        name="stickbreak",
    )(psb, psb, psb, gain)


def _out_mlp_kernel(x_ref, ya_ref, yb_ref, wo_ref, g2_ref, wu_ref, wd_ref, gf_ref, o_ref, *, final):
    x1 = (x_ref[...] + _dot(ya_ref[...], wo_ref[:RWKV_WIDTH, :])
          + _dot(yb_ref[...], wo_ref[RWKV_WIDTH:, :]))
    h = _bf(_rms(x1) * g2_ref[...])
    o_ref[...] = x1
    for c in range(D_FF // FF_CHUNK):
        cs = slice(c * FF_CHUNK, (c + 1) * FF_CHUNK)
        u = jnp.maximum(_dot(h, wu_ref[:, cs]), 0.0)
        o_ref[...] += _dot(_bf(u * u), wd_ref[cs, :])
    if final:
        o_ref[...] = _rms(o_ref[...]) * gf_ref[...]


def _out_mlp(x2d, ya, yb, wo, g2, wu, wd, gf, final):
    n = x2d.shape[0]
    const = lambda shape: pl.BlockSpec(shape, lambda i: (0, 0), pipeline_mode=pl.Buffered(1))
    return pl.pallas_call(
        functools.partial(_out_mlp_kernel, final=final),
        grid=(n // ROW_TILE,),
        in_specs=[
            pl.BlockSpec((ROW_TILE, D_MODEL), lambda i: (i, 0)),
            pl.BlockSpec((ROW_TILE, RWKV_WIDTH), lambda i: (i, 0)),
            pl.BlockSpec((ROW_TILE, SB_WIDTH), lambda i: (i, 0)),
            const((D_MODEL, D_MODEL)),
            const((1, D_MODEL)),
            const((D_MODEL, D_FF)),
            const((D_FF, D_MODEL)),
            const((1, D_MODEL)),
        ],
        out_specs=pl.BlockSpec((ROW_TILE, D_MODEL), lambda i: (i, 0)),
        out_shape=jax.ShapeDtypeStruct((n, D_MODEL), F32),
        compiler_params=pltpu.CompilerParams(
            dimension_semantics=("arbitrary",), vmem_limit_bytes=VMEM_LIMIT),
        name="outproj_mlp",
    )(x2d, ya, yb, wo, g2, wu, wd, gf)


def _lora_weight(w_decay_up, w_aaa_up, w_gate_up):
    z = lambda r: jnp.zeros((r, RWKV_WIDTH), F32)
    return _bf(jnp.concatenate([
        jnp.concatenate([w_decay_up, z(DECAY_LORA), z(DECAY_LORA)], axis=1),
        jnp.concatenate([z(AAA_LORA), w_aaa_up, z(AAA_LORA)], axis=1),
        jnp.concatenate([z(GATE_LORA), z(GATE_LORA), w_gate_up], axis=1),
    ], axis=0))


def kernel(x, ln1_g, w_in, tok_mu, w0, w_decay_up, a0, w_aaa_up, w_gate_up, k_k, k_a, r_k,
           gn_w, gn_b, sb_gain, w_out, ln2_g, w_up, w_down, lnf_g):
    bsz, t, d = x.shape
    assert d == D_MODEL and t % Q_BLOCK == 0 and (bsz * t) % ROW_TILE == 0
    depth = ln1_g.shape[0]
    row = lambda a: a.reshape(1, -1).astype(F32)
    x2d = x.reshape(bsz * t, d)
    for l in range(depth):
        pr, psb = _inproj(x2d, row(ln1_g[l]), _bf(w_in[l]))
        y_rwkv = _rwkv(pr.reshape(bsz, t, RWKV_COLS), row(tok_mu[l]), row(w0[l]), row(a0[l]),
                       row(k_k[l]), row(k_a[l]), row(r_k[l]), row(gn_w[l]), row(gn_b[l]),
                       _lora_weight(w_decay_up[l], w_aaa_up[l], w_gate_up[l]))
        y_sb = _sb(psb.reshape(bsz, t, SB_COLS), row(sb_gain[l]))
        x2d = _out_mlp(x2d, y_rwkv.reshape(bsz * t, RWKV_WIDTH), y_sb.reshape(bsz * t, SB_WIDTH),
                       _bf(w_out[l]), row(ln2_g[l]), _bf(w_up[l]), _bf(w_down[l]), row(lnf_g),
                       final=(l == depth - 1))
    return x2d.reshape(bsz, t, d)
```

```python
import functools

import jax
import jax.numpy as jnp
from jax import lax
from jax.experimental import pallas as pl
from jax.experimental.pallas import tpu as pltpu

D_MODEL = 1024
HEAD_DIM = 64
RWKV_WIDTH = 512
SB_WIDTH = 512
DECAY_LORA = 64
AAA_LORA = 64
GATE_LORA = 128
LORA_IN = DECAY_LORA + AAA_LORA + GATE_LORA
RWKV_COLS = 3 * RWKV_WIDTH + LORA_IN
SB_COLS = 3 * SB_WIDTH
IN_COLS = RWKV_COLS + SB_COLS
D_FF = 4 * D_MODEL
RMS_EPS = 1e-5
GN_EPS = 64e-5
KK_NORM_FLOOR = 1e-12
DECAY_SCALE = -0.6065306597126334

LANES = 128
PAIRS = RWKV_WIDTH // LANES
CHUNK = 64
Q_BLOCK = 128
INPROJ_COLS = 256
ROW_TILE = 512
FF_CHUNK = 1024
RWKV_SEQS = 4
LOG2E = 1.4426950408889634
Q_SCALE = HEAD_DIM ** -0.5 * LOG2E
EXP2_UNDERFLOW = -151.0
SB_SEQS = 4
VMEM_LIMIT = 56 * 1024 * 1024

F32 = jnp.float32
BF16 = jnp.bfloat16


def _bf(x):
    return x.astype(BF16)


def _dot(a, b):
    return jnp.dot(a, b, preferred_element_type=F32)


def _dot_nt(a, b):
    return lax.dot_general(a, b, (((1,), (1,)), ((), ())), preferred_element_type=F32)


def _split(x):
    hi = _bf(x)
    lo = _bf(x - hi.astype(F32))
    return hi, lo


def _iota(shape, axis):
    return lax.broadcasted_iota(jnp.int32, shape, axis)


def _head_block_matrix(value):
    same = (_iota((LANES, LANES), 0) >> 6) == (_iota((LANES, LANES), 1) >> 6)
    return jnp.where(same, value, 0.0).astype(BF16)


def _head_sum(x, block):
    cols = [_dot(_bf(x[:, j * LANES:(j + 1) * LANES]), block) for j in range(x.shape[1] // LANES)]
    return cols[0] if len(cols) == 1 else jnp.concatenate(cols, axis=1)


def _sigmoid(x):
    return 1.0 / (1.0 + jnp.exp2(x * -LOG2E))


def _rms(x):
    return x * lax.rsqrt(jnp.mean(x * x, axis=-1, keepdims=True) + RMS_EPS)


def _inproj_kernel(x_ref, g_ref, mu_ref, w_ref, pr_ref, psb_ref, carry_ref, *, tiles_per_seq):
    @pl.when(lax.rem(pl.program_id(0), tiles_per_seq) == 0)
    def _():
        carry_ref[...] = jnp.zeros_like(carry_ref)

    h = _bf(_rms(x_ref[...]) * g_ref[...])
    first_row = _iota((8, INPROJ_COLS), 0) == 0
    lane_l = _iota((ROW_TILE, LORA_IN), 1)
    for c in range(RWKV_COLS // INPROJ_COLS):
        cs = slice(c * INPROJ_COLS, (c + 1) * INPROJ_COLS)
        p = _dot(h, w_ref[:, cs])
        rolled = pltpu.roll(p, 1, 0)
        prev = jnp.concatenate([jnp.where(first_row, carry_ref[:, cs], rolled[:8]), rolled[8:]], axis=0)
        carry_ref[:, cs] = p[ROW_TILE - 1:ROW_TILE]
        p = p + mu_ref[:, cs] * (prev - p)
        if c == RWKV_COLS // INPROJ_COLS - 1:
            p = jnp.where(lane_l < DECAY_LORA, jnp.tanh(p),
                          jnp.where(lane_l < DECAY_LORA + AAA_LORA, p, _sigmoid(p)))
        pr_ref[:, cs] = _bf(p)
    psb = _dot(h, w_ref[:, RWKV_COLS:])
    psb_ref[:, :SB_WIDTH] = _bf(psb[:, :SB_WIDTH] * Q_SCALE)
    psb_ref[:, SB_WIDTH:] = _bf(psb[:, SB_WIDTH:])


def _inproj(x2d, g, mu, w_bf, seq_len):
    n = x2d.shape[0]
    return pl.pallas_call(
        functools.partial(_inproj_kernel, tiles_per_seq=seq_len // ROW_TILE),
        grid=(n // ROW_TILE,),
        in_specs=[
            pl.BlockSpec((ROW_TILE, D_MODEL), lambda i: (i, 0)),
            pl.BlockSpec((1, D_MODEL), lambda i: (0, 0)),
            pl.BlockSpec((1, RWKV_COLS), lambda i: (0, 0)),
            pl.BlockSpec((D_MODEL, IN_COLS), lambda i: (0, 0)),
        ],
        out_specs=[
            pl.BlockSpec((ROW_TILE, RWKV_COLS), lambda i: (i, 0)),
            pl.BlockSpec((ROW_TILE, SB_COLS), lambda i: (i, 0)),
        ],
        out_shape=[
            jax.ShapeDtypeStruct((n, RWKV_COLS), BF16),
            jax.ShapeDtypeStruct((n, SB_COLS), BF16),
        ],
        scratch_shapes=[pltpu.VMEM((1, RWKV_COLS), F32)],
        compiler_params=pltpu.CompilerParams(
            dimension_semantics=("arbitrary",), vmem_limit_bytes=VMEM_LIMIT),
        name="inproj",
    )(x2d, g, mu, w_bf)


def _rwkv_chains(ch, m):
    def blockdiag(y):
        return _bf(jnp.concatenate(
            [jnp.where(m["lane_lo"], y, 0.0), jnp.where(m["lane_lo"], 0.0, y)], axis=0))

    def each(fn, *lists):
        return [fn(*xs) for xs in zip(*lists)]

    at, rt, bt, kt = ([c[k] for c in ch] for k in ("at", "rt", "bt", "kt"))
    v, dl, s = ([c[k] for c in ch] for k in ("v", "dl", "s"))
    g = each(lambda a, r, b, k: _dot_nt(_bf(jnp.concatenate([a, r], axis=0)),
                                        jnp.concatenate([blockdiag(b), blockdiag(k)], axis=0)),
             at, rt, bt, kt)
    a_ab = [jnp.where(m["strict"], x[:CHUNK, :LANES], 0.0) for x in g]
    a_ak = [jnp.where(m["strict"], x[:CHUNK, LANES:], 0.0) for x in g]
    a_rb = [jnp.where(m["incl"], x[CHUNK:, :LANES], 0.0) for x in g]
    a_rk = [jnp.where(m["incl"], x[CHUNK:, LANES:], 0.0) for x in g]

    t = [m["eye"] + a for a in a_ab]
    x = each(lambda a: _dot(_bf(a), blockdiag(a)), a_ab)
    for _ in range(4):
        tx = each(lambda ti, xi: _dot(_bf(jnp.concatenate([ti, xi], axis=0)), blockdiag(xi)), t, x)
        t = each(lambda ti, txi: ti + txi[:CHUNK], t, tx)
        x = [txi[CHUNK:] for txi in tx]
    t = each(lambda ti, xi: ti + _dot(_bf(ti), blockdiag(xi)), t, x)

    v_bd = [blockdiag(vi) for vi in v]
    av = each(lambda a, vb: _dot(_bf(a), vb), a_ak, v_bd)
    wu = each(lambda ti, a, avi: _dot(_bf(ti), jnp.concatenate([blockdiag(a), blockdiag(avi)], axis=1)),
              t, at, av)
    w = [x_[:, :LANES] for x_ in wu]
    u0 = [x_[:, LANES:] for x_ in wu]
    ry = each(lambda a, wi, ui: _dot(_bf(a), jnp.concatenate([blockdiag(wi), blockdiag(ui)], axis=1)),
              a_rb, w, u0)
    yk = each(lambda a, vb: _dot(_bf(a), vb), a_rk, v_bd)
    rhat = each(lambda r, x_: r + x_[:, :LANES], rt, ry)
    y0 = each(lambda x_, k_: x_[:, LANES:] + k_, ry, yk)

    def transition(c, wi, ui, vi):
        bk_t = jnp.concatenate([c["bd"], c["kd"]], axis=0).T
        rhs = jnp.concatenate(
            [jnp.concatenate([wi, ui], axis=1),
             jnp.concatenate([jnp.zeros_like(vi), vi], axis=1)], axis=0)
        return _dot(_bf(bk_t), _bf(rhs))

    mn = each(transition, ch, w, u0, v)
    m_bd = each(lambda x_, d: jnp.where(m["same_head"], x_[:, :LANES], 0.0) + jnp.where(m["diag"], d, 0.0),
                mn, dl)
    n_bd = [jnp.where(m["same_head"], x_[:, LANES:], 0.0) for x_ in mn]
    rs = each(lambda r, mb, si: _dot(_bf(jnp.concatenate([r, mb], axis=0)), _bf(si)), rhat, m_bd, s)
    y = each(lambda x_, y_: x_[:CHUNK] + y_, rs, y0)
    s_new = each(lambda x_, n_: x_[CHUNK:] + n_, rs, n_bd)
    return y, s_new


def _rwkv_kernel(p_ref, w0_ref, a0_ref, kk_ref, ka_ref, rk_ref, gw_ref, gb_ref, wl_ref,
                 y_ref, state_ref, *, seqs):
    @pl.when(pl.program_id(1) == 0)
    def _():
        state_ref[...] = jnp.zeros_like(state_ref)

    rows = seqs * CHUNK
    w = RWKV_WIDTH
    p = jnp.concatenate([p_ref[b] for b in range(seqs)], axis=0)
    r = p[:, :w].astype(F32)
    k = p[:, w:2 * w].astype(F32)
    v = p[:, 2 * w:3 * w].astype(F32)
    lora_in = p[:, 3 * w:]
    lora = _dot(lora_in, wl_ref[...])
    logd = DECAY_SCALE * _sigmoid(w0_ref[...] + lora[:, :w])
    asig = _sigmoid(a0_ref[...] + lora[:, w:2 * w])
    gate = lora[:, 2 * w:]

    ones_blk = _head_block_matrix(1.0)
    mean_blk = _head_block_matrix(1.0 / HEAD_DIM)
    kkr = k * kk_ref[...]
    kk = kkr * jnp.minimum(lax.rsqrt(_head_sum(kkr * kkr, ones_blk)), 1.0 / KK_NORM_FLOOR)
    kmod = k * (1.0 + (asig - 1.0) * ka_ref[...])
    a_vec = -kk
    b_vec = kk * asig
    bonus = _head_sum(r * kmod * rk_ref[...], ones_blk) * v

    row_t = _iota((CHUNK, LANES), 0)
    col_i = _iota((CHUNK, LANES), 1) & (HEAD_DIM - 1)
    row_l = _iota((LANES, LANES), 0)
    col_l = _iota((LANES, LANES), 1)
    masks = {
        "lane_lo": _iota((CHUNK, LANES), 1) < HEAD_DIM,
        "strict": row_t > col_i,
        "incl": row_t >= col_i,
        "eye": jnp.where(row_t == col_i, 1.0, 0.0).astype(F32),
        "same_head": (row_l >> 6) == (col_l >> 6),
        "diag": row_l == col_l,
    }
    tri = jnp.where(_iota((CHUNK, CHUNK), 0) >= _iota((CHUNK, CHUNK), 1), 1.0, 0.0).astype(BF16)

    chains = []
    for b in range(seqs):
        sl = slice(b * CHUNK, (b + 1) * CHUNK)
        ld = logd[sl]
        hi, lo = _split(ld)
        cl = _dot(tri, hi) + _dot(tri, lo)
        e_pos = jnp.exp(cl)
        e_neg = jnp.exp(-cl)
        e_prev = jnp.exp(cl - ld)
        dl = e_pos[CHUNK - 1:CHUNK]
        at = a_vec[sl] * e_prev
        rt = r[sl] * e_pos
        bt = b_vec[sl] * e_neg
        kt = kmod[sl] * e_neg
        bd_ = bt * dl
        kd_ = kt * dl
        vb = v[sl]
        for pp in range(PAIRS):
            ls = slice(pp * LANES, (pp + 1) * LANES)
            chains.append(dict(at=at[:, ls], rt=rt[:, ls], bt=bt[:, ls], kt=kt[:, ls], bd=bd_[:, ls],
                               kd=kd_[:, ls], v=vb[:, ls], dl=dl[:, ls], s=state_ref[b, pp]))
    y_chains, s_chains = _rwkv_chains(chains, masks)
    for b in range(seqs):
        for pp in range(PAIRS):
            state_ref[b, pp] = s_chains[b * PAIRS + pp]
    y = jnp.concatenate(
        [jnp.concatenate(y_chains[b * PAIRS:(b + 1) * PAIRS], axis=1) for b in range(seqs)], axis=0)

    mean = _head_sum(y, mean_blk)
    yc = y - mean
    var = _head_sum(yc * yc, mean_blk)
    yn = yc * lax.rsqrt(var + GN_EPS) * gw_ref[...] + gb_ref[...]
    out = _bf((yn + bonus) * gate)
    for b in range(seqs):
        y_ref[b] = out[b * CHUNK:(b + 1) * CHUNK]


def _rwkv(pr, w0, a0, k_k, k_a, r_k, gn_w, gn_b, w_lora):
    bsz, t, _ = pr.shape
    seqs = RWKV_SEQS if bsz % RWKV_SEQS == 0 else 1
    vec = lambda c: pl.BlockSpec((1, c), lambda b, j: (0, 0))
    return pl.pallas_call(
        functools.partial(_rwkv_kernel, seqs=seqs),
        grid=(bsz // seqs, t // CHUNK),
        in_specs=[
            pl.BlockSpec((seqs, CHUNK, RWKV_COLS), lambda b, j: (b, j, 0)),
            vec(RWKV_WIDTH), vec(RWKV_WIDTH), vec(RWKV_WIDTH), vec(RWKV_WIDTH),
            vec(RWKV_WIDTH), vec(RWKV_WIDTH), vec(RWKV_WIDTH),
            pl.BlockSpec((LORA_IN, 3 * RWKV_WIDTH), lambda b, j: (0, 0)),
        ],
        out_specs=pl.BlockSpec((seqs, CHUNK, RWKV_WIDTH), lambda b, j: (b, j, 0)),
        out_shape=jax.ShapeDtypeStruct((bsz, t, RWKV_WIDTH), BF16),
        scratch_shapes=[
            pltpu.VMEM((seqs, PAIRS, LANES, LANES), F32),
        ],
        compiler_params=pltpu.CompilerParams(
            dimension_semantics=("arbitrary", "arbitrary"), vmem_limit_bytes=VMEM_LIMIT),
        name="rwkv7",
    )(pr, w0, a0, k_k, k_a, r_k, gn_w, gn_b, w_lora)


def _sb_kernel(q_ref, k_ref, v_ref, gain_ref, o_ref, c_ref, acc_ref, *, seqs):
    qi = pl.program_id(1)
    half = Q_BLOCK // 2
    units = [(s, u, h) for s in range(seqs) for u in range(PAIRS) for h in range(2)]
    lanes = lambda u: slice(u * LANES, (u + 1) * LANES)
    lane_lo = _iota((half, LANES), 1) < HEAD_DIM
    q2 = []
    for s, u, h in units:
        q = q_ref[s, h * half:(h + 1) * half, lanes(u)]
        zero = jnp.zeros_like(q)
        q2.append(jnp.concatenate([jnp.where(lane_lo, q, zero), jnp.where(lane_lo, zero, q)], axis=0))

    row_q = _iota((Q_BLOCK, LANES), 0) & (half - 1)
    col_k = _iota((Q_BLOCK, LANES), 1)
    neg_cum = jnp.where(
        (_iota((LANES, 2 * LANES), 0) > _iota((LANES, 2 * LANES), 1))
        | (_iota((LANES, 2 * LANES), 1) >= LANES), -1.0, 0.0).astype(BF16)
    q_start = qi * Q_BLOCK

    def softplus2(x):
        return jnp.maximum(x, 0.0) + jnp.log2(1.0 + jnp.exp2(-jnp.abs(x)))

    def tile(starts, visible, first):
        kj = [[k_ref[s, pl.ds(starts[h], Q_BLOCK), :] for h in range(2)] for s in range(seqs)]
        vj = [[v_ref[s, pl.ds(starts[h], Q_BLOCK), :] for h in range(2)] for s in range(seqs)]
        z = [_dot_nt(q2[i], kj[s][h][:, lanes(u)]) for i, (s, u, h) in enumerate(units)]
        sp = [softplus2(x) for x in z]
        cs = [_dot(_bf(x if visible[h] is None else jnp.where(visible[h], x, 0.0)), neg_cum)
              for x, (s, u, h) in zip(sp, units)]
        cmax = None
        for i, (s, u, h) in enumerate(units):
            log_a = z[i] - sp[i] + cs[i][:, :LANES]
            if not first:
                log_a = log_a + c_ref[i]
            att = jnp.exp2(log_a)
            if visible[h] is not None:
                att = jnp.where(visible[h], att, 0.0)
            pv = _dot(_bf(att), vj[s][h][:, lanes(u)])
            if first:
                acc_ref[i] = pv
                c_new = cs[i][:, LANES:]
            else:
                acc_ref[i] += pv
                c_new = c_ref[i] + cs[i][:, LANES:]
            c_ref[i] = c_new
            cmax = c_new if cmax is None else jnp.maximum(cmax, c_new)
        return jnp.max(cmax)

    start0 = [jnp.maximum(q_start + (h - 1) * half, 0) for h in range(2)]
    first_visible = [col_k < row_q + (q_start + h * half - start0[h]) for h in range(2)]
    cmax0 = tile([pl.multiple_of(x, half) for x in start0], first_visible, True)

    def cond(carry):
        i, cmax = carry
        return jnp.logical_and(i < qi, cmax >= EXP2_UNDERFLOW)

    def body(carry):
        i, _ = carry
        starts = [pl.multiple_of(q_start + (h - 1) * half - i * Q_BLOCK, half) for h in range(2)]
        return i + 1, tile(starts, [None, None], False)

    i_end, cmax_end = lax.while_loop(cond, body, (jnp.int32(1), cmax0))

    @pl.when(jnp.logical_and(jnp.logical_and(qi >= 1, i_end == qi), cmax_end >= EXP2_UNDERFLOW))
    def _():
        tile([0, 0], [col_k < half, None], False)

    mean_blk = _head_block_matrix(1.0 / HEAD_DIM)
    for s in range(seqs):
        rows = []
        for h in range(2):
            acc = [acc_ref[(s * PAIRS + u) * 2 + h] for u in range(PAIRS)]
            rows.append(jnp.concatenate([jnp.where(lane_lo, a[:half], a[half:]) for a in acc], axis=1))
        o = jnp.concatenate(rows, axis=0)
        ms = _head_sum(o * o, mean_blk)
        o_ref[s] = _bf(o * lax.rsqrt(ms + RMS_EPS) * gain_ref[...])


def _sb(psb, gain):
    bsz, t, _ = psb.shape
    seqs = SB_SEQS if bsz % SB_SEQS == 0 else 1
    return pl.pallas_call(
        functools.partial(_sb_kernel, seqs=seqs),
        grid=(bsz // seqs, t // Q_BLOCK),
        in_specs=[
            pl.BlockSpec((seqs, Q_BLOCK, SB_WIDTH), lambda b, i: (b, i, 0)),
            pl.BlockSpec((seqs, t, SB_WIDTH), lambda b, i: (b, 0, 1)),
            pl.BlockSpec((seqs, t, SB_WIDTH), lambda b, i: (b, 0, 2)),
            pl.BlockSpec((1, SB_WIDTH), lambda b, i: (0, 0)),
        ],
        out_specs=pl.BlockSpec((seqs, Q_BLOCK, SB_WIDTH), lambda b, i: (b, i, 0)),
        out_shape=jax.ShapeDtypeStruct((bsz, t, SB_WIDTH), BF16),
        scratch_shapes=[
            pltpu.VMEM((2 * seqs * PAIRS, Q_BLOCK, LANES), F32),
            pltpu.VMEM((2 * seqs * PAIRS, Q_BLOCK, LANES), F32),
        ],
        compiler_params=pltpu.CompilerParams(
            dimension_semantics=("arbitrary", "arbitrary"), vmem_limit_bytes=VMEM_LIMIT),
        name="stickbreak",
    )(psb, psb, psb, gain)


def _out_mlp_kernel(x_ref, ya_ref, yb_ref, wo_ref, g2_ref, wu_ref, wd_ref, gf_ref, o_ref, *, final):
    x1 = (x_ref[...] + _dot(ya_ref[...], wo_ref[:RWKV_WIDTH, :])
          + _dot(yb_ref[...], wo_ref[RWKV_WIDTH:, :]))
    h = _bf(_rms(x1) * g2_ref[...])
    o_ref[...] = x1
    for c in range(D_FF // FF_CHUNK):
        cs = slice(c * FF_CHUNK, (c + 1) * FF_CHUNK)
        u = jnp.maximum(_dot(h, wu_ref[:, cs]), 0.0)
        o_ref[...] += _dot(_bf(u * u), wd_ref[cs, :])
    if final:
        o_ref[...] = _rms(o_ref[...]) * gf_ref[...]


def _out_mlp(x2d, ya, yb, wo, g2, wu, wd, gf, final):
    n = x2d.shape[0]
    const = lambda shape: pl.BlockSpec(shape, lambda i: (0, 0), pipeline_mode=pl.Buffered(1))
    return pl.pallas_call(
        functools.partial(_out_mlp_kernel, final=final),
        grid=(n // ROW_TILE,),
        in_specs=[
            pl.BlockSpec((ROW_TILE, D_MODEL), lambda i: (i, 0)),
            pl.BlockSpec((ROW_TILE, RWKV_WIDTH), lambda i: (i, 0)),
            pl.BlockSpec((ROW_TILE, SB_WIDTH), lambda i: (i, 0)),
            const((D_MODEL, D_MODEL)),
            const((1, D_MODEL)),
            const((D_MODEL, D_FF)),
            const((D_FF, D_MODEL)),
            const((1, D_MODEL)),
        ],
        out_specs=pl.BlockSpec((ROW_TILE, D_MODEL), lambda i: (i, 0)),
        out_shape=jax.ShapeDtypeStruct((n, D_MODEL), F32),
        compiler_params=pltpu.CompilerParams(
            dimension_semantics=("arbitrary",), vmem_limit_bytes=VMEM_LIMIT),
        name="outproj_mlp",
    )(x2d, ya, yb, wo, g2, wu, wd, gf)


def _lora_weight(w_decay_up, w_aaa_up, w_gate_up):
    z = lambda r: jnp.zeros((r, RWKV_WIDTH), F32)
    return _bf(jnp.concatenate([
        jnp.concatenate([w_decay_up, z(DECAY_LORA), z(DECAY_LORA)], axis=1),
        jnp.concatenate([z(AAA_LORA), w_aaa_up, z(AAA_LORA)], axis=1),
        jnp.concatenate([z(GATE_LORA), z(GATE_LORA), w_gate_up], axis=1),
    ], axis=0))


def kernel(x, ln1_g, w_in, tok_mu, w0, w_decay_up, a0, w_aaa_up, w_gate_up, k_k, k_a, r_k,
           gn_w, gn_b, sb_gain, w_out, ln2_g, w_up, w_down, lnf_g):
    bsz, t, d = x.shape
    assert d == D_MODEL and t % ROW_TILE == 0
    depth = ln1_g.shape[0]
    row = lambda a: a.reshape(1, -1).astype(F32)
    x2d = x.reshape(bsz * t, d)
    for l in range(depth):
        pr, psb = _inproj(x2d, row(ln1_g[l]), row(tok_mu[l]), _bf(w_in[l]), t)
        y_rwkv = _rwkv(pr.reshape(bsz, t, RWKV_COLS), row(w0[l]), row(a0[l]),
                       row(k_k[l]), row(k_a[l]), row(r_k[l]), row(gn_w[l]), row(gn_b[l]),
                       _lora_weight(w_decay_up[l], w_aaa_up[l], w_gate_up[l]))
        y_sb = _sb(psb.reshape(bsz, t, SB_COLS), row(sb_gain[l]))
        x2d = _out_mlp(x2d, y_rwkv.reshape(bsz * t, RWKV_WIDTH), y_sb.reshape(bsz * t, SB_WIDTH),
                       _bf(w_out[l]), row(ln2_g[l]), _bf(w_up[l]), _bf(w_down[l]), row(lnf_g),
                       final=(l == depth - 1))
    return x2d.reshape(bsz, t, d)
```

```python
import functools

import jax
import jax.numpy as jnp
from jax import lax
from jax.experimental import pallas as pl
from jax.experimental.pallas import tpu as pltpu

D_MODEL = 1024
HEAD_DIM = 64
RWKV_WIDTH = 512
SB_WIDTH = 512
DECAY_LORA = 64
AAA_LORA = 64
GATE_LORA = 128
LORA_IN = DECAY_LORA + AAA_LORA + GATE_LORA
RWKV_COLS = 3 * RWKV_WIDTH + LORA_IN
SB_COLS = 3 * SB_WIDTH
IN_COLS = RWKV_COLS + SB_COLS
D_FF = 4 * D_MODEL
RMS_EPS = 1e-5
GN_EPS = 64e-5
KK_NORM_FLOOR = 1e-12
DECAY_SCALE = -0.6065306597126334

LANES = 128
MXU_COLS = 256
PAIRS = RWKV_WIDTH // LANES
CHUNK = 64
Q_BLOCK = 128
ROW_TILE = 512
FF_CHUNK = 1024
RWKV_SEQS = 4
LOG2E = 1.4426950408889634
Q_SCALE = HEAD_DIM ** -0.5 * LOG2E
EXP2_UNDERFLOW = -151.0
SB_SEQS = 4
VMEM_LIMIT = 56 * 1024 * 1024

F32 = jnp.float32
BF16 = jnp.bfloat16


def _bf(x):
    return x.astype(BF16)


def _dot(a, b):
    return jnp.dot(a, b, preferred_element_type=F32)


def _dot_nt(a, b):
    return lax.dot_general(a, b, (((1,), (1,)), ((), ())), preferred_element_type=F32)


def _split(x):
    hi = _bf(x)
    lo = _bf(x - hi.astype(F32))
    return hi, lo


def _iota(shape, axis):
    return lax.broadcasted_iota(jnp.int32, shape, axis)


def _head_block_matrix(value, width=LANES):
    same = (_iota((width, width), 0) >> 6) == (_iota((width, width), 1) >> 6)
    return jnp.where(same, value, 0.0).astype(BF16)


def _head_sum(x, block):
    width = block.shape[0]
    cols = [_dot(_bf(x[:, j * width:(j + 1) * width]), block) for j in range(x.shape[1] // width)]
    return cols[0] if len(cols) == 1 else jnp.concatenate(cols, axis=1)


def _sigmoid(x):
    return 1.0 / (1.0 + jnp.exp2(x * -LOG2E))


def _rms(x):
    return x * lax.rsqrt(jnp.mean(x * x, axis=-1, keepdims=True) + RMS_EPS)


def _inproj_kernel(x_ref, g_ref, mu_ref, w_ref, wl_ref, w0_ref, a0_ref, kk_ref, ka_ref, rk_ref,
                   at_ref, rt_ref, bt_ref, kt_ref, v_ref, gate_ref, bonus_ref, dl_ref, psb_ref, carry_ref,
                   *, tiles_per_seq):
    @pl.when(lax.rem(pl.program_id(0), tiles_per_seq) == 0)
    def _():
        carry_ref[...] = jnp.zeros_like(carry_ref)

    h = _bf(_rms(x_ref[...]) * g_ref[...])
    first_row = _iota((8, MXU_COLS), 0) == 0

    def project_sb(lo):
        p = _dot(h, w_ref[:, RWKV_COLS + lo:RWKV_COLS + lo + MXU_COLS])
        psb_ref[:, lo:lo + MXU_COLS] = _bf(p * Q_SCALE if lo < SB_WIDTH else p)

    def project(lo):
        cs = slice(lo, lo + MXU_COLS)
        p = _dot(h, w_ref[:, cs])
        rolled = pltpu.roll(p, 1, 0)
        prev = jnp.concatenate([jnp.where(first_row, carry_ref[:, cs], rolled[:8]), rolled[8:]], axis=0)
        carry_ref[:, cs] = p[ROW_TILE - 1:ROW_TILE]
        return p + mu_ref[:, cs] * (prev - p)

    w = RWKV_WIDTH
    assert LORA_IN == MXU_COLS
    groups = w // MXU_COLS
    sb_slices = iter(range(0, SB_COLS, MXU_COLS))
    ones_blk = _head_block_matrix(1.0, MXU_COLS)
    tri = jnp.where(_iota((CHUNK, CHUNK), 0) >= _iota((CHUNK, CHUNK), 1), 1.0, 0.0).astype(BF16)
    tri2 = jnp.concatenate([tri, tri], axis=1)
    lane_l = _iota((ROW_TILE, LORA_IN), 1)

    def frames(j):
        lo_ = j * MXU_COLS
        return project(lo_), project(w + lo_), project(2 * w + lo_)

    def sums(j, rkv, lora):
        ls = slice(j * MXU_COLS, (j + 1) * MXU_COLS)
        r, k, v = rkv
        logd = DECAY_SCALE * _sigmoid(w0_ref[:, ls] + lora[:, ls])
        asig = _sigmoid(a0_ref[:, ls] + lora[:, w + j * MXU_COLS:w + (j + 1) * MXU_COLS])
        gate_ref[:, ls] = _bf(lora[:, 2 * w + j * MXU_COLS:2 * w + (j + 1) * MXU_COLS])
        kkr = k * kk_ref[:, ls]
        kmod = k * (1.0 + (asig - 1.0) * ka_ref[:, ls])
        ss = _head_sum(kkr * kkr, ones_blk)
        bonus_s = _head_sum(r * kmod * rk_ref[:, ls], ones_blk)
        hi, lo = _split(logd)
        cl = jnp.concatenate(
            [_dot(tri2, jnp.concatenate([hi[c * CHUNK:(c + 1) * CHUNK], lo[c * CHUNK:(c + 1) * CHUNK]], axis=0))
             for c in range(ROW_TILE // CHUNK)], axis=0)
        return ls, r, v, logd, asig, kkr, kmod, ss, bonus_s, cl

    def operands(ls, r, v, logd, asig, kkr, kmod, ss, bonus_s, cl):
        kk = kkr * jnp.minimum(lax.rsqrt(ss), 1.0 / KK_NORM_FLOOR)
        bonus_ref[:, ls] = _bf(bonus_s * v)
        v_ref[:, ls] = _bf(v)
        e_pos = jnp.exp(cl)
        e_neg = jnp.exp(-cl)
        at_ref[:, ls] = _bf(-kk * jnp.exp(cl - logd))
        rt_ref[:, ls] = _bf(r * e_pos)
        bt_ref[:, ls] = _bf(kk * asig * e_neg)
        kt_ref[:, ls] = _bf(kmod * e_neg)
        for c in range(ROW_TILE // CHUNK):
            dl_ref[c:c + 1, ls] = e_pos[(c + 1) * CHUNK - 1:(c + 1) * CHUNK]

    xl = project(3 * w)
    rkv = frames(0)
    lora_in = jnp.where(lane_l < DECAY_LORA, jnp.tanh(xl),
                        jnp.where(lane_l < DECAY_LORA + AAA_LORA, xl, _sigmoid(xl)))
    lora = _dot(_bf(lora_in), wl_ref[...])
    for j in range(groups):
        project_sb(next(sb_slices))
        nxt = frames(j + 1) if j + 1 < groups else None
        staged = sums(j, rkv, lora)
        project_sb(next(sb_slices))
        operands(*staged)
        rkv = nxt
    for lo_ in sb_slices:
        project_sb(lo_)


def _inproj(x2d, g, mu, w_bf, w_lora, w0, a0, k_k, k_a, r_k, seq_len):
    n = x2d.shape[0]
    chunks_per_tile = ROW_TILE // CHUNK
    const = lambda shape: pl.BlockSpec(shape, lambda i: (0, 0))
    wide = lambda: pl.BlockSpec((ROW_TILE, RWKV_WIDTH), lambda i: (i, 0))
    return pl.pallas_call(
        functools.partial(_inproj_kernel, tiles_per_seq=seq_len // ROW_TILE),
        grid=(n // ROW_TILE,),
        in_specs=[
            pl.BlockSpec((ROW_TILE, D_MODEL), lambda i: (i, 0)),
            const((1, D_MODEL)), const((1, RWKV_COLS)), const((D_MODEL, IN_COLS)),
            const((LORA_IN, 3 * RWKV_WIDTH)),
            const((1, RWKV_WIDTH)), const((1, RWKV_WIDTH)), const((1, RWKV_WIDTH)), const((1, RWKV_WIDTH)),
            const((1, RWKV_WIDTH)),
        ],
        out_specs=[wide(), wide(), wide(), wide(), wide(), wide(), wide(),
                   pl.BlockSpec((chunks_per_tile, RWKV_WIDTH), lambda i: (i, 0)),
                   pl.BlockSpec((ROW_TILE, SB_COLS), lambda i: (i, 0))],
        out_shape=[jax.ShapeDtypeStruct((n, RWKV_WIDTH), BF16)] * 7 + [
            jax.ShapeDtypeStruct((n // CHUNK, RWKV_WIDTH), F32),
            jax.ShapeDtypeStruct((n, SB_COLS), BF16),
        ],
        scratch_shapes=[pltpu.VMEM((1, RWKV_COLS), F32)],
        compiler_params=pltpu.CompilerParams(
            dimension_semantics=("arbitrary",), vmem_limit_bytes=VMEM_LIMIT),
        name="inproj",
    )(x2d, g, mu, w_bf, w_lora, w0, a0, k_k, k_a, r_k)


def _rwkv_chains(ch, m):
    def blockdiag(y):
        yb = _bf(y)
        zero = jnp.zeros_like(yb)
        return jnp.concatenate([jnp.where(m["lane_lo"], yb, zero), jnp.where(m["lane_lo"], zero, yb)], axis=0)

    def each(fn, *lists):
        return [fn(*xs) for xs in zip(*lists)]

    at, rt, bt, kt = ([c[k] for c in ch] for k in ("at", "rt", "bt", "kt"))
    v, dl, s = ([c[k] for c in ch] for k in ("v", "dl", "s"))
    g = each(lambda a, r, b, k: _dot_nt(jnp.concatenate([a, r], axis=0),
                                        jnp.concatenate([blockdiag(b), blockdiag(k)], axis=0)),
             at, rt, bt, kt)
    a_ab = [jnp.where(m["strict"], x[:CHUNK, :LANES], 0.0) for x in g]
    a_ak = [jnp.where(m["strict"], x[:CHUNK, LANES:], 0.0) for x in g]
    a_rb = [jnp.where(m["incl"], x[CHUNK:, :LANES], 0.0) for x in g]
    a_rk = [jnp.where(m["incl"], x[CHUNK:, LANES:], 0.0) for x in g]

    t = [m["eye"] + a for a in a_ab]
    x = each(lambda a: _dot(_bf(a), blockdiag(a)), a_ab)
    for _ in range(4):
        tx = each(lambda ti, xi: _dot(_bf(jnp.concatenate([ti, xi], axis=0)), blockdiag(xi)), t, x)
        t = each(lambda ti, txi: ti + txi[:CHUNK], t, tx)
        x = [txi[CHUNK:] for txi in tx]
    t = each(lambda ti, xi: ti + _dot(_bf(ti), blockdiag(xi)), t, x)

    v_bd = [blockdiag(vi) for vi in v]
    av = each(lambda a, vb: _dot(_bf(a), vb), a_ak, v_bd)
    wu = each(lambda ti, a, avi: _dot(_bf(ti), jnp.concatenate([blockdiag(a), blockdiag(avi)], axis=1)),
              t, at, av)
    w = [x_[:, :LANES] for x_ in wu]
    u0 = [x_[:, LANES:] for x_ in wu]
    ry = each(lambda a, wi, ui: _dot(_bf(a), jnp.concatenate([blockdiag(wi), blockdiag(ui)], axis=1)),
              a_rb, w, u0)
    yk = each(lambda a, vb: _dot(_bf(a), vb), a_rk, v_bd)
    rhat = each(lambda r, x_: r.astype(F32) + x_[:, :LANES], rt, ry)
    y0 = each(lambda x_, k_: x_[:, LANES:] + k_, ry, yk)

    def transition(b, k, d, wi, ui, vi):
        bk_t = jnp.concatenate([b.astype(F32) * d, k.astype(F32) * d], axis=0).T
        rhs = jnp.concatenate(
            [jnp.concatenate([_bf(wi), _bf(ui)], axis=1),
             jnp.concatenate([jnp.zeros_like(vi), vi], axis=1)], axis=0)
        return _dot(_bf(bk_t), rhs)

    mn = each(transition, bt, kt, dl, w, u0, v)
    m_bd = each(lambda x_, d: jnp.where(m["same_head"], x_[:, :LANES], 0.0) + jnp.where(m["diag"], d, 0.0),
                mn, dl)
    n_bd = [jnp.where(m["same_head"], x_[:, LANES:], 0.0) for x_ in mn]
    rs = each(lambda r, mb, si: _dot(_bf(jnp.concatenate([r, mb], axis=0)), _bf(si)), rhat, m_bd, s)
    y = each(lambda x_, y_: x_[:CHUNK] + y_, rs, y0)
    s_new = each(lambda x_, n_: x_[CHUNK:] + n_, rs, n_bd)
    return y, s_new


def _rwkv_kernel(at_ref, rt_ref, bt_ref, kt_ref, v_ref, gate_ref, bonus_ref, dl_ref, gw_ref, gb_ref,
                 y_ref, state_ref, *, seqs):
    @pl.when(pl.program_id(1) == 0)
    def _():
        state_ref[...] = jnp.zeros_like(state_ref)

    row_t = _iota((CHUNK, LANES), 0)
    col_i = _iota((CHUNK, LANES), 1) & (HEAD_DIM - 1)
    row_l = _iota((LANES, LANES), 0)
    col_l = _iota((LANES, LANES), 1)
    masks = {
        "lane_lo": _iota((CHUNK, LANES), 1) < HEAD_DIM,
        "strict": row_t > col_i,
        "incl": row_t >= col_i,
        "eye": jnp.where(row_t == col_i, 1.0, 0.0).astype(F32),
        "same_head": (row_l >> 6) == (col_l >> 6),
        "diag": row_l == col_l,
    }

    chains = []
    for b in range(seqs):
        for pp in range(PAIRS):
            ls = slice(pp * LANES, (pp + 1) * LANES)
            chains.append(dict(at=at_ref[b, :, ls], rt=rt_ref[b, :, ls], bt=bt_ref[b, :, ls],
                               kt=kt_ref[b, :, ls], v=v_ref[b, :, ls], dl=dl_ref[b, :, ls],
                               s=state_ref[b, pp]))
    y_chains, s_chains = _rwkv_chains(chains, masks)
    for b in range(seqs):
        for pp in range(PAIRS):
            state_ref[b, pp] = s_chains[b * PAIRS + pp]
    y = jnp.concatenate(
        [jnp.concatenate(y_chains[b * PAIRS:(b + 1) * PAIRS], axis=1) for b in range(seqs)], axis=0)

    mean_blk = _head_block_matrix(1.0 / HEAD_DIM, MXU_COLS)
    mean = _head_sum(y, mean_blk)
    yc = y - mean
    var = _head_sum(yc * yc, mean_blk)
    yn = yc * lax.rsqrt(var + GN_EPS) * gw_ref[...] + gb_ref[...]
    bonus = jnp.concatenate([bonus_ref[b] for b in range(seqs)], axis=0).astype(F32)
    gate = jnp.concatenate([gate_ref[b] for b in range(seqs)], axis=0).astype(F32)
    out = _bf((yn + bonus) * gate)
    for b in range(seqs):
        y_ref[b] = out[b * CHUNK:(b + 1) * CHUNK]


def _rwkv(at, rt, bt, kt, v, gate, bonus, dl, gn_w, gn_b):
    bsz, t, _ = at.shape
    seqs = RWKV_SEQS if bsz % RWKV_SEQS == 0 else 1
    vec = lambda c: pl.BlockSpec((1, c), lambda b, j: (0, 0))
    frames = lambda: pl.BlockSpec((seqs, CHUNK, RWKV_WIDTH), lambda b, j: (b, j, 0))
    return pl.pallas_call(
        functools.partial(_rwkv_kernel, seqs=seqs),
        grid=(bsz // seqs, t // CHUNK),
        in_specs=[frames(), frames(), frames(), frames(), frames(), frames(), frames(),
                  pl.BlockSpec((seqs, None, 1, RWKV_WIDTH), lambda b, j: (b, j, 0, 0)),
                  vec(RWKV_WIDTH), vec(RWKV_WIDTH)],
        out_specs=frames(),
        out_shape=jax.ShapeDtypeStruct((bsz, t, RWKV_WIDTH), BF16),
        scratch_shapes=[pltpu.VMEM((seqs, PAIRS, LANES, LANES), F32)],
        compiler_params=pltpu.CompilerParams(
            dimension_semantics=("arbitrary", "arbitrary"), vmem_limit_bytes=VMEM_LIMIT),
        name="rwkv7",
    )(at, rt, bt, kt, v, gate, bonus, dl, gn_w, gn_b)


def _sb_kernel(q_ref, k_ref, v_ref, gain_ref, o_ref, c_ref, acc_ref, *, seqs):
    qi = pl.program_id(1)
    half = Q_BLOCK // 2
    units = [(s, u, h) for s in range(seqs) for u in range(PAIRS) for h in range(2)]
    lanes = lambda u: slice(u * LANES, (u + 1) * LANES)
    lane_lo = _iota((half, LANES), 1) < HEAD_DIM
    q2 = []
    for s, u, h in units:
        q = q_ref[s, h * half:(h + 1) * half, lanes(u)]
        zero = jnp.zeros_like(q)
        q2.append(jnp.concatenate([jnp.where(lane_lo, q, zero), jnp.where(lane_lo, zero, q)], axis=0))

    row_q = _iota((Q_BLOCK, LANES), 0) & (half - 1)
    col_k = _iota((Q_BLOCK, LANES), 1)
    neg_cum = jnp.where(
        (_iota((LANES, 2 * LANES), 0) > _iota((LANES, 2 * LANES), 1))
        | (_iota((LANES, 2 * LANES), 1) >= LANES), -1.0, 0.0).astype(BF16)
    q_start = qi * Q_BLOCK

    def softplus2(x):
        return jnp.maximum(x, 0.0) + jnp.log2(1.0 + jnp.exp2(-jnp.abs(x)))

    def tile(starts, visible, first):
        kj = [[k_ref[s, pl.ds(starts[h], Q_BLOCK), :] for h in range(2)] for s in range(seqs)]
        vj = [[v_ref[s, pl.ds(starts[h], Q_BLOCK), :] for h in range(2)] for s in range(seqs)]
        z = [_dot_nt(q2[i], kj[s][h][:, lanes(u)]) for i, (s, u, h) in enumerate(units)]
        sp = [softplus2(x) for x in z]
        cs = [_dot(_bf(x if visible[h] is None else jnp.where(visible[h], x, 0.0)), neg_cum)
              for x, (s, u, h) in zip(sp, units)]
        cmax = None
        for i, (s, u, h) in enumerate(units):
            log_a = z[i] - sp[i] + cs[i][:, :LANES]
            if not first:
                log_a = log_a + c_ref[i]
            att = jnp.exp2(log_a)
            if visible[h] is not None:
                att = jnp.where(visible[h], att, 0.0)
            pv = _dot(_bf(att), vj[s][h][:, lanes(u)])
            if first:
                acc_ref[i] = pv
                c_new = cs[i][:, LANES:]
            else:
                acc_ref[i] += pv
                c_new = c_ref[i] + cs[i][:, LANES:]
            c_ref[i] = c_new
            cmax = c_new if cmax is None else jnp.maximum(cmax, c_new)
        return jnp.max(cmax)

    start0 = [jnp.maximum(q_start + (h - 1) * half, 0) for h in range(2)]
    first_visible = [col_k < row_q + (q_start + h * half - start0[h]) for h in range(2)]
    cmax0 = tile([pl.multiple_of(x, half) for x in start0], first_visible, True)

    def cond(carry):
        i, cmax = carry
        return jnp.logical_and(i < qi, cmax >= EXP2_UNDERFLOW)

    def body(carry):
        i, _ = carry
        starts = [pl.multiple_of(q_start + (h - 1) * half - i * Q_BLOCK, half) for h in range(2)]
        return i + 1, tile(starts, [None, None], False)

    i_end, cmax_end = lax.while_loop(cond, body, (jnp.int32(1), cmax0))

    @pl.when(jnp.logical_and(jnp.logical_and(qi >= 1, i_end == qi), cmax_end >= EXP2_UNDERFLOW))
    def _():
        tile([0, 0], [col_k < half, None], False)

    mean_blk = _head_block_matrix(1.0 / HEAD_DIM)
    for s in range(seqs):
        rows = []
        for h in range(2):
            acc = [acc_ref[(s * PAIRS + u) * 2 + h] for u in range(PAIRS)]
            rows.append(jnp.concatenate([jnp.where(lane_lo, a[:half], a[half:]) for a in acc], axis=1))
        o = jnp.concatenate(rows, axis=0)
        ms = _head_sum(o * o, mean_blk)
        o_ref[s] = _bf(o * lax.rsqrt(ms + RMS_EPS) * gain_ref[...])


def _sb(psb, gain):
    bsz, t, _ = psb.shape
    seqs = SB_SEQS if bsz % SB_SEQS == 0 else 1
    return pl.pallas_call(
        functools.partial(_sb_kernel, seqs=seqs),
        grid=(bsz // seqs, t // Q_BLOCK),
        in_specs=[
            pl.BlockSpec((seqs, Q_BLOCK, SB_WIDTH), lambda b, i: (b, i, 0)),
            pl.BlockSpec((seqs, t, SB_WIDTH), lambda b, i: (b, 0, 1)),
            pl.BlockSpec((seqs, t, SB_WIDTH), lambda b, i: (b, 0, 2)),
            pl.BlockSpec((1, SB_WIDTH), lambda b, i: (0, 0)),
        ],
        out_specs=pl.BlockSpec((seqs, Q_BLOCK, SB_WIDTH), lambda b, i: (b, i, 0)),
        out_shape=jax.ShapeDtypeStruct((bsz, t, SB_WIDTH), BF16),
        scratch_shapes=[
            pltpu.VMEM((2 * seqs * PAIRS, Q_BLOCK, LANES), F32),
            pltpu.VMEM((2 * seqs * PAIRS, Q_BLOCK, LANES), F32),
        ],
        compiler_params=pltpu.CompilerParams(
            dimension_semantics=("arbitrary", "arbitrary"), vmem_limit_bytes=VMEM_LIMIT),
        name="stickbreak",
    )(psb, psb, psb, gain)


def _out_mlp_kernel(x_ref, ya_ref, yb_ref, wo_ref, g2_ref, wu_ref, wd_ref, gf_ref, o_ref, *, final):
    x1 = (x_ref[...] + _dot(ya_ref[...], wo_ref[:RWKV_WIDTH, :])
          + _dot(yb_ref[...], wo_ref[RWKV_WIDTH:, :]))
    h = _bf(_rms(x1) * g2_ref[...])
    o_ref[...] = x1
    for c in range(D_FF // FF_CHUNK):
        cs = slice(c * FF_CHUNK, (c + 1) * FF_CHUNK)
        u = jnp.maximum(_dot(h, wu_ref[:, cs]), 0.0)
        o_ref[...] += _dot(_bf(u * u), wd_ref[cs, :])
    if final:
        o_ref[...] = _rms(o_ref[...]) * gf_ref[...]


def _out_mlp(x2d, ya, yb, wo, g2, wu, wd, gf, final):
    n = x2d.shape[0]
    const = lambda shape: pl.BlockSpec(shape, lambda i: (0, 0), pipeline_mode=pl.Buffered(1))
    return pl.pallas_call(
        functools.partial(_out_mlp_kernel, final=final),
        grid=(n // ROW_TILE,),
        in_specs=[
            pl.BlockSpec((ROW_TILE, D_MODEL), lambda i: (i, 0)),
            pl.BlockSpec((ROW_TILE, RWKV_WIDTH), lambda i: (i, 0)),
            pl.BlockSpec((ROW_TILE, SB_WIDTH), lambda i: (i, 0)),
            const((D_MODEL, D_MODEL)),
            const((1, D_MODEL)),
            const((D_MODEL, D_FF)),
            const((D_FF, D_MODEL)),
            const((1, D_MODEL)),
        ],
        out_specs=pl.BlockSpec((ROW_TILE, D_MODEL), lambda i: (i, 0)),
        out_shape=jax.ShapeDtypeStruct((n, D_MODEL), F32),
        compiler_params=pltpu.CompilerParams(
            dimension_semantics=("arbitrary",), vmem_limit_bytes=VMEM_LIMIT),
        name="outproj_mlp",
    )(x2d, ya, yb, wo, g2, wu, wd, gf)


def _lora_weight(w_decay_up, w_aaa_up, w_gate_up):
    z = lambda r: jnp.zeros((r, RWKV_WIDTH), F32)
    return _bf(jnp.concatenate([
        jnp.concatenate([w_decay_up, z(DECAY_LORA), z(DECAY_LORA)], axis=1),
        jnp.concatenate([z(AAA_LORA), w_aaa_up, z(AAA_LORA)], axis=1),
        jnp.concatenate([z(GATE_LORA), z(GATE_LORA), w_gate_up], axis=1),
    ], axis=0))


def kernel(x, ln1_g, w_in, tok_mu, w0, w_decay_up, a0, w_aaa_up, w_gate_up, k_k, k_a, r_k,
           gn_w, gn_b, sb_gain, w_out, ln2_g, w_up, w_down, lnf_g):
    bsz, t, d = x.shape
    assert d == D_MODEL and t % ROW_TILE == 0
    depth = ln1_g.shape[0]
    row = lambda a: a.reshape(1, -1).astype(F32)
    x2d = x.reshape(bsz * t, d)
    for l in range(depth):
        *frames, dl, psb = _inproj(x2d, row(ln1_g[l]), row(tok_mu[l]), _bf(w_in[l]),
                                   _lora_weight(w_decay_up[l], w_aaa_up[l], w_gate_up[l]),
                                   row(w0[l]), row(a0[l]), row(k_k[l]), row(k_a[l]), row(r_k[l]), t)
        y_rwkv = _rwkv(*[f.reshape(bsz, t, RWKV_WIDTH) for f in frames],
                       dl.reshape(bsz, t // CHUNK, 1, RWKV_WIDTH), row(gn_w[l]), row(gn_b[l]))
        y_sb = _sb(psb.reshape(bsz, t, SB_COLS), row(sb_gain[l]))
        x2d = _out_mlp(x2d, y_rwkv.reshape(bsz * t, RWKV_WIDTH), y_sb.reshape(bsz * t, SB_WIDTH),
                       _bf(w_out[l]), row(ln2_g[l]), _bf(w_up[l]), _bf(w_down[l]), row(lnf_g),
                       final=(l == depth - 1))
    return x2d.reshape(bsz, t, d)
```

```python
import functools

import jax
import jax.numpy as jnp
from jax import lax
from jax.experimental import pallas as pl
from jax.experimental.pallas import tpu as pltpu

D_MODEL = 1024
HEAD_DIM = 64
RWKV_WIDTH = 512
SB_WIDTH = 512
DECAY_LORA = 64
AAA_LORA = 64
GATE_LORA = 128
LORA_IN = DECAY_LORA + AAA_LORA + GATE_LORA
RWKV_COLS = 3 * RWKV_WIDTH + LORA_IN
SB_COLS = 3 * SB_WIDTH
IN_COLS = RWKV_COLS + SB_COLS
D_FF = 4 * D_MODEL
RMS_EPS = 1e-5
GN_EPS = 64e-5
KK_NORM_FLOOR = 1e-12
DECAY_SCALE = -0.6065306597126334

LANES = 128
MXU_COLS = 256
PAIRS = RWKV_WIDTH // LANES
CHUNK = 64
Q_BLOCK = 128
ROW_TILE = 512
MLP_ROW_TILE = 1024
FF_CHUNK = 1024
RWKV_SEQS = 8
LOG2E = 1.4426950408889634
Q_SCALE = HEAD_DIM ** -0.5 * LOG2E
EXP2_UNDERFLOW = -151.0
SB_SEQS = 4
VMEM_LIMIT = 56 * 1024 * 1024

F32 = jnp.float32
BF16 = jnp.bfloat16


def _bf(x):
    return x.astype(BF16)


def _dot(a, b):
    return jnp.dot(a, b, preferred_element_type=F32)


def _dot_nt(a, b):
    return lax.dot_general(a, b, (((1,), (1,)), ((), ())), preferred_element_type=F32)


def _split(x):
    hi = _bf(x)
    lo = _bf(x - hi.astype(F32))
    return hi, lo


def _iota(shape, axis):
    return lax.broadcasted_iota(jnp.int32, shape, axis)


def _head_block_matrix(value, width=LANES):
    same = (_iota((width, width), 0) >> 6) == (_iota((width, width), 1) >> 6)
    return jnp.where(same, value, 0.0).astype(BF16)


def _head_sum(x, block):
    width = block.shape[0]
    cols = [_dot(_bf(x[:, j * width:(j + 1) * width]), block) for j in range(x.shape[1] // width)]
    return cols[0] if len(cols) == 1 else jnp.concatenate(cols, axis=1)


def _sigmoid(x):
    return 1.0 / (1.0 + jnp.exp2(x * -LOG2E))


def _rms(x):
    return x * lax.rsqrt(jnp.mean(x * x, axis=-1, keepdims=True) + RMS_EPS)


def _inproj_kernel(x_ref, g_ref, mu_ref, w_ref, wl_ref, w0_ref, a0_ref, kk_ref, ka_ref, rk_ref,
                   at_ref, rt_ref, bt_ref, kt_ref, v_ref, gate_ref, bonus_ref, dl_ref, psb_ref, carry_ref,
                   *, tiles_per_seq):
    @pl.when(lax.rem(pl.program_id(0), tiles_per_seq) == 0)
    def _():
        carry_ref[...] = jnp.zeros_like(carry_ref)

    h = _bf(_rms(x_ref[...]) * g_ref[...])
    first_row = _iota((8, MXU_COLS), 0) == 0

    def project_sb(lo):
        p = _dot(h, w_ref[:, RWKV_COLS + lo:RWKV_COLS + lo + MXU_COLS])
        psb_ref[:, lo:lo + MXU_COLS] = _bf(p * Q_SCALE if lo < SB_WIDTH else p)

    def project(lo):
        cs = slice(lo, lo + MXU_COLS)
        p = _dot(h, w_ref[:, cs])
        rolled = pltpu.roll(p, 1, 0)
        prev = jnp.concatenate([jnp.where(first_row, carry_ref[:, cs], rolled[:8]), rolled[8:]], axis=0)
        carry_ref[:, cs] = p[ROW_TILE - 1:ROW_TILE]
        return p + mu_ref[:, cs] * (prev - p)

    w = RWKV_WIDTH
    assert LORA_IN == MXU_COLS
    groups = w // MXU_COLS
    sb_slices = iter(range(0, SB_COLS, MXU_COLS))
    ones_blk = _head_block_matrix(1.0, MXU_COLS)
    tri = jnp.where(_iota((CHUNK, CHUNK), 0) >= _iota((CHUNK, CHUNK), 1), 1.0, 0.0).astype(BF16)
    tri2 = jnp.concatenate([tri, tri], axis=1)
    lane_l = _iota((ROW_TILE, LORA_IN), 1)

    def frames(j):
        lo_ = j * MXU_COLS
        return project(lo_), project(w + lo_), project(2 * w + lo_)

    def sums(j, rkv, lora):
        ls = slice(j * MXU_COLS, (j + 1) * MXU_COLS)
        r, k, v = rkv
        logd = DECAY_SCALE * _sigmoid(w0_ref[:, ls] + lora[:, ls])
        asig = _sigmoid(a0_ref[:, ls] + lora[:, w + j * MXU_COLS:w + (j + 1) * MXU_COLS])
        gate_ref[:, ls] = _bf(lora[:, 2 * w + j * MXU_COLS:2 * w + (j + 1) * MXU_COLS])
        kkr = k * kk_ref[:, ls]
        kmod = k * (1.0 + (asig - 1.0) * ka_ref[:, ls])
        ss = _head_sum(kkr * kkr, ones_blk)
        bonus_s = _head_sum(r * kmod * rk_ref[:, ls], ones_blk)
        hi, lo = _split(logd)
        cl = jnp.concatenate(
            [_dot(tri2, jnp.concatenate([hi[c * CHUNK:(c + 1) * CHUNK], lo[c * CHUNK:(c + 1) * CHUNK]], axis=0))
             for c in range(ROW_TILE // CHUNK)], axis=0)
        return ls, r, v, logd, asig, kkr, kmod, ss, bonus_s, cl

    def operands(ls, r, v, logd, asig, kkr, kmod, ss, bonus_s, cl):
        kk = kkr * jnp.minimum(lax.rsqrt(ss), 1.0 / KK_NORM_FLOOR)
        bonus_ref[:, ls] = _bf(bonus_s * v)
        v_ref[:, ls] = _bf(v)
        e_pos = jnp.exp(cl)
        e_neg = jnp.exp(-cl)
        at_ref[:, ls] = _bf(-kk * jnp.exp(cl - logd))
        rt_ref[:, ls] = _bf(r * e_pos)
        bt_ref[:, ls] = _bf(kk * asig * e_neg)
        kt_ref[:, ls] = _bf(kmod * e_neg)
        for c in range(ROW_TILE // CHUNK):
            dl_ref[c:c + 1, ls] = e_pos[(c + 1) * CHUNK - 1:(c + 1) * CHUNK]

    xl = project(3 * w)
    rkv = frames(0)
    lora_in = jnp.where(lane_l < DECAY_LORA, jnp.tanh(xl),
                        jnp.where(lane_l < DECAY_LORA + AAA_LORA, xl, _sigmoid(xl)))
    lora = _dot(_bf(lora_in), wl_ref[...])
    for j in range(groups):
        project_sb(next(sb_slices))
        nxt = frames(j + 1) if j + 1 < groups else None
        staged = sums(j, rkv, lora)
        project_sb(next(sb_slices))
        operands(*staged)
        rkv = nxt
    for lo_ in sb_slices:
        project_sb(lo_)


def _inproj(x2d, g, mu, w_bf, w_lora, w0, a0, k_k, k_a, r_k, seq_len):
    n = x2d.shape[0]
    chunks_per_tile = ROW_TILE // CHUNK
    const = lambda shape: pl.BlockSpec(shape, lambda i: (0, 0))
    wide = lambda: pl.BlockSpec((ROW_TILE, RWKV_WIDTH), lambda i: (i, 0))
    return pl.pallas_call(
        functools.partial(_inproj_kernel, tiles_per_seq=seq_len // ROW_TILE),
        grid=(n // ROW_TILE,),
        in_specs=[
            pl.BlockSpec((ROW_TILE, D_MODEL), lambda i: (i, 0)),
            const((1, D_MODEL)), const((1, RWKV_COLS)), const((D_MODEL, IN_COLS)),
            const((LORA_IN, 3 * RWKV_WIDTH)),
            const((1, RWKV_WIDTH)), const((1, RWKV_WIDTH)), const((1, RWKV_WIDTH)), const((1, RWKV_WIDTH)),
            const((1, RWKV_WIDTH)),
        ],
        out_specs=[wide(), wide(), wide(), wide(), wide(), wide(), wide(),
                   pl.BlockSpec((chunks_per_tile, RWKV_WIDTH), lambda i: (i, 0)),
                   pl.BlockSpec((ROW_TILE, SB_COLS), lambda i: (i, 0))],
        out_shape=[jax.ShapeDtypeStruct((n, RWKV_WIDTH), BF16)] * 7 + [
            jax.ShapeDtypeStruct((n // CHUNK, RWKV_WIDTH), F32),
            jax.ShapeDtypeStruct((n, SB_COLS), BF16),
        ],
        scratch_shapes=[pltpu.VMEM((1, RWKV_COLS), F32)],
        compiler_params=pltpu.CompilerParams(
            dimension_semantics=("arbitrary",), vmem_limit_bytes=VMEM_LIMIT),
        name="inproj",
    )(x2d, g, mu, w_bf, w_lora, w0, a0, k_k, k_a, r_k)


def _rwkv_chains(ch, m):
    def blockdiag(y):
        yb = _bf(y)
        zero = jnp.zeros_like(yb)
        return jnp.concatenate([jnp.where(m["lane_lo"], yb, zero), jnp.where(m["lane_lo"], zero, yb)], axis=0)

    def each(fn, *lists):
        return [fn(*xs) for xs in zip(*lists)]

    at, rt, bt, kt = ([c[k] for c in ch] for k in ("at", "rt", "bt", "kt"))
    v, dl, s = ([c[k] for c in ch] for k in ("v", "dl", "s"))
    g = each(lambda a, r, b, k: _dot_nt(jnp.concatenate([a, r], axis=0),
                                        jnp.concatenate([blockdiag(b), blockdiag(k)], axis=0)),
             at, rt, bt, kt)
    a_ab = [jnp.where(m["strict"], x[:CHUNK, :LANES], 0.0) for x in g]
    a_ak = [jnp.where(m["strict"], x[:CHUNK, LANES:], 0.0) for x in g]
    a_rb = [jnp.where(m["incl"], x[CHUNK:, :LANES], 0.0) for x in g]
    a_rk = [jnp.where(m["incl"], x[CHUNK:, LANES:], 0.0) for x in g]

    t = [m["eye"] + a for a in a_ab]
    x = each(lambda a: _dot(_bf(a), blockdiag(a)), a_ab)
    for _ in range(4):
        tx = each(lambda ti, xi: _dot(_bf(jnp.concatenate([ti, xi], axis=0)), blockdiag(xi)), t, x)
        t = each(lambda ti, txi: ti + txi[:CHUNK], t, tx)
        x = [txi[CHUNK:] for txi in tx]
    t = each(lambda ti, xi: ti + _dot(_bf(ti), blockdiag(xi)), t, x)

    v_bd = [blockdiag(vi) for vi in v]
    av = each(lambda a, vb: _dot(_bf(a), vb), a_ak, v_bd)
    wu = each(lambda ti, a, avi: _dot(_bf(ti), jnp.concatenate([blockdiag(a), blockdiag(avi)], axis=1)),
              t, at, av)
    w = [x_[:, :LANES] for x_ in wu]
    u0 = [x_[:, LANES:] for x_ in wu]
    ry = each(lambda a, wi, ui: _dot(_bf(a), jnp.concatenate([blockdiag(wi), blockdiag(ui)], axis=1)),
              a_rb, w, u0)
    yk = each(lambda a, vb: _dot(_bf(a), vb), a_rk, v_bd)
    rhat = each(lambda r, x_: r.astype(F32) + x_[:, :LANES], rt, ry)
    y0 = each(lambda x_, k_: x_[:, LANES:] + k_, ry, yk)

    def transition(b, k, d, wi, ui, vi):
        bk_t = jnp.concatenate([b.astype(F32) * d, k.astype(F32) * d], axis=0).T
        rhs = jnp.concatenate(
            [jnp.concatenate([_bf(wi), _bf(ui)], axis=1),
             jnp.concatenate([jnp.zeros_like(vi), vi], axis=1)], axis=0)
        return _dot(_bf(bk_t), rhs)

    mn = each(transition, bt, kt, dl, w, u0, v)
    m_bd = each(lambda x_, d: jnp.where(m["same_head"], x_[:, :LANES], 0.0) + jnp.where(m["diag"], d, 0.0),
                mn, dl)
    n_bd = [jnp.where(m["same_head"], x_[:, LANES:], 0.0) for x_ in mn]
    rs = each(lambda r, mb, si: _dot(_bf(jnp.concatenate([r, mb], axis=0)), _bf(si)), rhat, m_bd, s)
    y = each(lambda x_, y_: x_[:CHUNK] + y_, rs, y0)
    s_new = each(lambda x_, n_: x_[CHUNK:] + n_, rs, n_bd)
    return y, s_new


def _rwkv_kernel(at_ref, rt_ref, bt_ref, kt_ref, v_ref, gate_ref, bonus_ref, dl_ref, gw_ref, gb_ref,
                 y_ref, state_ref, *, seqs):
    @pl.when(pl.program_id(1) == 0)
    def _():
        state_ref[...] = jnp.zeros_like(state_ref)

    row_t = _iota((CHUNK, LANES), 0)
    col_i = _iota((CHUNK, LANES), 1) & (HEAD_DIM - 1)
    row_l = _iota((LANES, LANES), 0)
    col_l = _iota((LANES, LANES), 1)
    masks = {
        "lane_lo": _iota((CHUNK, LANES), 1) < HEAD_DIM,
        "strict": row_t > col_i,
        "incl": row_t >= col_i,
        "eye": jnp.where(row_t == col_i, 1.0, 0.0).astype(F32),
        "same_head": (row_l >> 6) == (col_l >> 6),
        "diag": row_l == col_l,
    }

    chains = []
    for b in range(seqs):
        for pp in range(PAIRS):
            ls = slice(pp * LANES, (pp + 1) * LANES)
            chains.append(dict(at=at_ref[b, :, ls], rt=rt_ref[b, :, ls], bt=bt_ref[b, :, ls],
                               kt=kt_ref[b, :, ls], v=v_ref[b, :, ls], dl=dl_ref[b, :, ls],
                               s=state_ref[b, pp]))
    y_chains, s_chains = _rwkv_chains(chains, masks)
    for b in range(seqs):
        for pp in range(PAIRS):
            state_ref[b, pp] = s_chains[b * PAIRS + pp]
    y = jnp.concatenate(
        [jnp.concatenate(y_chains[b * PAIRS:(b + 1) * PAIRS], axis=1) for b in range(seqs)], axis=0)

    mean_blk = _head_block_matrix(1.0 / HEAD_DIM, MXU_COLS)
    mean = _head_sum(y, mean_blk)
    yc = y - mean
    var = _head_sum(yc * yc, mean_blk)
    yn = yc * lax.rsqrt(var + GN_EPS) * gw_ref[...] + gb_ref[...]
    bonus = jnp.concatenate([bonus_ref[b] for b in range(seqs)], axis=0).astype(F32)
    gate = jnp.concatenate([gate_ref[b] for b in range(seqs)], axis=0).astype(F32)
    out = _bf((yn + bonus) * gate)
    for b in range(seqs):
        y_ref[b] = out[b * CHUNK:(b + 1) * CHUNK]


def _rwkv(at, rt, bt, kt, v, gate, bonus, dl, gn_w, gn_b):
    bsz, t, _ = at.shape
    seqs = RWKV_SEQS if bsz % RWKV_SEQS == 0 else 1
    vec = lambda c: pl.BlockSpec((1, c), lambda b, j: (0, 0))
    frames = lambda: pl.BlockSpec((seqs, CHUNK, RWKV_WIDTH), lambda b, j: (b, j, 0))
    return pl.pallas_call(
        functools.partial(_rwkv_kernel, seqs=seqs),
        grid=(bsz // seqs, t // CHUNK),
        in_specs=[frames(), frames(), frames(), frames(), frames(), frames(), frames(),
                  pl.BlockSpec((seqs, None, 1, RWKV_WIDTH), lambda b, j: (b, j, 0, 0)),
                  vec(RWKV_WIDTH), vec(RWKV_WIDTH)],
        out_specs=frames(),
        out_shape=jax.ShapeDtypeStruct((bsz, t, RWKV_WIDTH), BF16),
        scratch_shapes=[pltpu.VMEM((seqs, PAIRS, LANES, LANES), F32)],
        compiler_params=pltpu.CompilerParams(
            dimension_semantics=("arbitrary", "arbitrary"), vmem_limit_bytes=VMEM_LIMIT),
        name="rwkv7",
    )(at, rt, bt, kt, v, gate, bonus, dl, gn_w, gn_b)


def _sb_kernel(q_ref, k_ref, v_ref, gain_ref, o_ref, c_ref, acc_ref, *, seqs):
    qi = pl.program_id(1)
    half = Q_BLOCK // 2
    units = [(s, u, h) for s in range(seqs) for u in range(PAIRS) for h in range(2)]
    lanes = lambda u: slice(u * LANES, (u + 1) * LANES)
    lane_lo = _iota((half, LANES), 1) < HEAD_DIM
    q2 = []
    for s, u, h in units:
        q = q_ref[s, h * half:(h + 1) * half, lanes(u)]
        zero = jnp.zeros_like(q)
        q2.append(jnp.concatenate([jnp.where(lane_lo, q, zero), jnp.where(lane_lo, zero, q)], axis=0))

    row_q = _iota((Q_BLOCK, LANES), 0) & (half - 1)
    col_k = _iota((Q_BLOCK, LANES), 1)
    neg_cum = jnp.where(
        (_iota((LANES, 2 * LANES), 0) > _iota((LANES, 2 * LANES), 1))
        | (_iota((LANES, 2 * LANES), 1) >= LANES), -1.0, 0.0).astype(BF16)
    q_start = qi * Q_BLOCK

    def softplus2(x):
        return jnp.maximum(x, 0.0) + jnp.log2(1.0 + jnp.exp2(-jnp.abs(x)))

    def tile(starts, visible, first):
        kj = [[k_ref[s, pl.ds(starts[h], Q_BLOCK), :] for h in range(2)] for s in range(seqs)]
        vj = [[v_ref[s, pl.ds(starts[h], Q_BLOCK), :] for h in range(2)] for s in range(seqs)]
        z = [_dot_nt(q2[i], kj[s][h][:, lanes(u)]) for i, (s, u, h) in enumerate(units)]
        sp = [softplus2(x) for x in z]
        cs = [_dot(_bf(x if visible[h] is None else jnp.where(visible[h], x, 0.0)), neg_cum)
              for x, (s, u, h) in zip(sp, units)]
        cmax = None
        for i, (s, u, h) in enumerate(units):
            log_a = z[i] - sp[i] + cs[i][:, :LANES]
            if not first:
                log_a = log_a + c_ref[i]
            att = jnp.exp2(log_a)
            if visible[h] is not None:
                att = jnp.where(visible[h], att, 0.0)
            pv = _dot(_bf(att), vj[s][h][:, lanes(u)])
            if first:
                acc_ref[i] = pv
                c_new = cs[i][:, LANES:]
            else:
                acc_ref[i] += pv
                c_new = c_ref[i] + cs[i][:, LANES:]
            c_ref[i] = c_new
            cmax = c_new if cmax is None else jnp.maximum(cmax, c_new)
        return jnp.max(cmax)

    start0 = [jnp.maximum(q_start + (h - 1) * half, 0) for h in range(2)]
    first_visible = [col_k < row_q + (q_start + h * half - start0[h]) for h in range(2)]
    cmax0 = tile([pl.multiple_of(x, half) for x in start0], first_visible, True)

    def cond(carry):
        i, cmax = carry
        return jnp.logical_and(i < qi, cmax >= EXP2_UNDERFLOW)

    def body(carry):
        i, _ = carry
        starts = [pl.multiple_of(q_start + (h - 1) * half - i * Q_BLOCK, half) for h in range(2)]
        return i + 1, tile(starts, [None, None], False)

    i_end, cmax_end = lax.while_loop(cond, body, (jnp.int32(1), cmax0))

    @pl.when(jnp.logical_and(jnp.logical_and(qi >= 1, i_end == qi), cmax_end >= EXP2_UNDERFLOW))
    def _():
        tile([0, 0], [col_k < half, None], False)

    mean_blk = _head_block_matrix(1.0 / HEAD_DIM)
    for s in range(seqs):
        rows = []
        for h in range(2):
            acc = [acc_ref[(s * PAIRS + u) * 2 + h] for u in range(PAIRS)]
            rows.append(jnp.concatenate([jnp.where(lane_lo, a[:half], a[half:]) for a in acc], axis=1))
        o = jnp.concatenate(rows, axis=0)
        ms = _head_sum(o * o, mean_blk)
        o_ref[s] = _bf(o * lax.rsqrt(ms + RMS_EPS) * gain_ref[...])


def _sb(psb, gain):
    bsz, t, _ = psb.shape
    seqs = SB_SEQS if bsz % SB_SEQS == 0 else 1
    return pl.pallas_call(
        functools.partial(_sb_kernel, seqs=seqs),
        grid=(bsz // seqs, t // Q_BLOCK),
        in_specs=[
            pl.BlockSpec((seqs, Q_BLOCK, SB_WIDTH), lambda b, i: (b, i, 0)),
            pl.BlockSpec((seqs, t, SB_WIDTH), lambda b, i: (b, 0, 1)),
            pl.BlockSpec((seqs, t, SB_WIDTH), lambda b, i: (b, 0, 2)),
            pl.BlockSpec((1, SB_WIDTH), lambda b, i: (0, 0)),
        ],
        out_specs=pl.BlockSpec((seqs, Q_BLOCK, SB_WIDTH), lambda b, i: (b, i, 0)),
        out_shape=jax.ShapeDtypeStruct((bsz, t, SB_WIDTH), BF16),
        scratch_shapes=[
            pltpu.VMEM((2 * seqs * PAIRS, Q_BLOCK, LANES), F32),
            pltpu.VMEM((2 * seqs * PAIRS, Q_BLOCK, LANES), F32),
        ],
        compiler_params=pltpu.CompilerParams(
            dimension_semantics=("arbitrary", "arbitrary"), vmem_limit_bytes=VMEM_LIMIT),
        name="stickbreak",
    )(psb, psb, psb, gain)


def _out_mlp_kernel(x_ref, ya_ref, yb_ref, wo_ref, g2_ref, wu_ref, wd_ref, gf_ref, o_ref, *, final):
    x1 = (x_ref[...] + _dot(ya_ref[...], wo_ref[:RWKV_WIDTH, :])
          + _dot(yb_ref[...], wo_ref[RWKV_WIDTH:, :]))
    h = _bf(_rms(x1) * g2_ref[...])
    o_ref[...] = x1
    for c in range(D_FF // FF_CHUNK):
        cs = slice(c * FF_CHUNK, (c + 1) * FF_CHUNK)
        u = jnp.maximum(_dot(h, wu_ref[:, cs]), 0.0)
        o_ref[...] += _dot(_bf(u * u), wd_ref[cs, :])
    if final:
        o_ref[...] = _rms(o_ref[...]) * gf_ref[...]


def _out_mlp(x2d, ya, yb, wo, g2, wu, wd, gf, final):
    n = x2d.shape[0]
    const = lambda shape: pl.BlockSpec(shape, lambda i: (0, 0), pipeline_mode=pl.Buffered(1))
    return pl.pallas_call(
        functools.partial(_out_mlp_kernel, final=final),
        grid=(n // MLP_ROW_TILE,),
        in_specs=[
            pl.BlockSpec((MLP_ROW_TILE, D_MODEL), lambda i: (i, 0)),
            pl.BlockSpec((MLP_ROW_TILE, RWKV_WIDTH), lambda i: (i, 0)),
            pl.BlockSpec((MLP_ROW_TILE, SB_WIDTH), lambda i: (i, 0)),
            const((D_MODEL, D_MODEL)),
            const((1, D_MODEL)),
            const((D_MODEL, D_FF)),
            const((D_FF, D_MODEL)),
            const((1, D_MODEL)),
        ],
        out_specs=pl.BlockSpec((MLP_ROW_TILE, D_MODEL), lambda i: (i, 0)),
        out_shape=jax.ShapeDtypeStruct((n, D_MODEL), F32),
        compiler_params=pltpu.CompilerParams(
            dimension_semantics=("arbitrary",), vmem_limit_bytes=VMEM_LIMIT),
        name="outproj_mlp",
    )(x2d, ya, yb, wo, g2, wu, wd, gf)


def _lora_weight(w_decay_up, w_aaa_up, w_gate_up):
    z = lambda r: jnp.zeros((r, RWKV_WIDTH), F32)
    return _bf(jnp.concatenate([
        jnp.concatenate([w_decay_up, z(DECAY_LORA), z(DECAY_LORA)], axis=1),
        jnp.concatenate([z(AAA_LORA), w_aaa_up, z(AAA_LORA)], axis=1),
        jnp.concatenate([z(GATE_LORA), z(GATE_LORA), w_gate_up], axis=1),
    ], axis=0))


def kernel(x, ln1_g, w_in, tok_mu, w0, w_decay_up, a0, w_aaa_up, w_gate_up, k_k, k_a, r_k,
           gn_w, gn_b, sb_gain, w_out, ln2_g, w_up, w_down, lnf_g):
    bsz, t, d = x.shape
    assert d == D_MODEL and t % ROW_TILE == 0 and (bsz * t) % MLP_ROW_TILE == 0
    depth = ln1_g.shape[0]
    row = lambda a: a.reshape(1, -1).astype(F32)
    x2d = x.reshape(bsz * t, d)
    for l in range(depth):
        *frames, dl, psb = _inproj(x2d, row(ln1_g[l]), row(tok_mu[l]), _bf(w_in[l]),
                                   _lora_weight(w_decay_up[l], w_aaa_up[l], w_gate_up[l]),
                                   row(w0[l]), row(a0[l]), row(k_k[l]), row(k_a[l]), row(r_k[l]), t)
        y_rwkv = _rwkv(*[f.reshape(bsz, t, RWKV_WIDTH) for f in frames],
                       dl.reshape(bsz, t // CHUNK, 1, RWKV_WIDTH), row(gn_w[l]), row(gn_b[l]))
        y_sb = _sb(psb.reshape(bsz, t, SB_COLS), row(sb_gain[l]))
        x2d = _out_mlp(x2d, y_rwkv.reshape(bsz * t, RWKV_WIDTH), y_sb.reshape(bsz * t, SB_WIDTH),
                       _bf(w_out[l]), row(ln2_g[l]), _bf(w_up[l]), _bf(w_down[l]), row(lnf_g),
                       final=(l == depth - 1))
    return x2d.reshape(bsz, t, d)
```

```python
import functools

import jax
import jax.numpy as jnp
from jax import lax
from jax.experimental import pallas as pl
from jax.experimental.pallas import tpu as pltpu

D_MODEL = 1024
HEAD_DIM = 64
RWKV_WIDTH = 512
SB_WIDTH = 512
DECAY_LORA = 64
AAA_LORA = 64
GATE_LORA = 128
LORA_IN = DECAY_LORA + AAA_LORA + GATE_LORA
RWKV_COLS = 3 * RWKV_WIDTH + LORA_IN
SB_COLS = 3 * SB_WIDTH
IN_COLS = RWKV_COLS + SB_COLS
D_FF = 4 * D_MODEL
RMS_EPS = 1e-5
GN_EPS = 64e-5
KK_NORM_FLOOR = 1e-12
DECAY_SCALE = -0.6065306597126334

LANES = 128
MXU_COLS = 256
PAIRS = RWKV_WIDTH // LANES
CHUNK = 64
Q_BLOCK = 128
ROW_TILE = 512
MLP_ROW_TILE = 1024
FF_CHUNK = 1024
RWKV_SEQS = 8
LOG2E = 1.4426950408889634
Q_SCALE = HEAD_DIM ** -0.5 * LOG2E
EXP2_UNDERFLOW = -151.0
SB_SEQS = 4
VMEM_LIMIT = 56 * 1024 * 1024

F32 = jnp.float32
BF16 = jnp.bfloat16


def _bf(x):
    return x.astype(BF16)


def _dot(a, b):
    return jnp.dot(a, b, preferred_element_type=F32)


def _dot_nt(a, b):
    return lax.dot_general(a, b, (((1,), (1,)), ((), ())), preferred_element_type=F32)


def _split(x):
    hi = _bf(x)
    lo = _bf(x - hi.astype(F32))
    return hi, lo


def _iota(shape, axis):
    return lax.broadcasted_iota(jnp.int32, shape, axis)


def _head_block_matrix(value, width=LANES):
    same = (_iota((width, width), 0) >> 6) == (_iota((width, width), 1) >> 6)
    return jnp.where(same, value, 0.0).astype(BF16)


def _head_sum(x, block):
    width = block.shape[0]
    cols = [_dot(_bf(x[:, j * width:(j + 1) * width]), block) for j in range(x.shape[1] // width)]
    return cols[0] if len(cols) == 1 else jnp.concatenate(cols, axis=1)


def _sigmoid(x):
    return 1.0 / (1.0 + jnp.exp2(x * -LOG2E))


def _rms(x):
    return x * lax.rsqrt(jnp.mean(x * x, axis=-1, keepdims=True) + RMS_EPS)


def _inproj_kernel(x_ref, g_ref, mu_ref, w_ref, wl_ref, w0_ref, a0_ref, kk_ref, ka_ref, rk_ref,
                   at_ref, rt_ref, bt_ref, kt_ref, v_ref, gate_ref, bonus_ref, dl_ref, psb_ref, carry_ref,
                   *, tiles_per_seq):
    @pl.when(lax.rem(pl.program_id(0), tiles_per_seq) == 0)
    def _():
        carry_ref[...] = jnp.zeros_like(carry_ref)

    h = _bf(_rms(x_ref[...]) * g_ref[...])
    first_row = _iota((8, MXU_COLS), 0) == 0

    def project_sb(lo):
        p = _dot(h, w_ref[:, RWKV_COLS + lo:RWKV_COLS + lo + MXU_COLS])
        psb_ref[:, lo:lo + MXU_COLS] = _bf(p * Q_SCALE if lo < SB_WIDTH else p)

    def project(lo):
        cs = slice(lo, lo + MXU_COLS)
        p = _dot(h, w_ref[:, cs])
        rolled = pltpu.roll(p, 1, 0)
        prev = jnp.concatenate([jnp.where(first_row, carry_ref[:, cs], rolled[:8]), rolled[8:]], axis=0)
        carry_ref[:, cs] = p[ROW_TILE - 1:ROW_TILE]
        return p + mu_ref[:, cs] * (prev - p)

    w = RWKV_WIDTH
    assert LORA_IN == MXU_COLS
    groups = w // MXU_COLS
    sb_slices = iter(range(0, SB_COLS, MXU_COLS))
    ones_blk = _head_block_matrix(1.0, MXU_COLS)
    tri = jnp.where(_iota((CHUNK, CHUNK), 0) >= _iota((CHUNK, CHUNK), 1), 1.0, 0.0).astype(BF16)
    tri2 = jnp.concatenate([tri, tri], axis=1)
    lane_l = _iota((ROW_TILE, LORA_IN), 1)

    def frames(j):
        lo_ = j * MXU_COLS
        return project(lo_), project(w + lo_), project(2 * w + lo_)

    def sums(j, rkv, lora):
        ls = slice(j * MXU_COLS, (j + 1) * MXU_COLS)
        r, k, v = rkv
        logd = DECAY_SCALE * _sigmoid(w0_ref[:, ls] + lora[:, ls])
        asig = _sigmoid(a0_ref[:, ls] + lora[:, w + j * MXU_COLS:w + (j + 1) * MXU_COLS])
        gate_ref[:, ls] = _bf(lora[:, 2 * w + j * MXU_COLS:2 * w + (j + 1) * MXU_COLS])
        kkr = k * kk_ref[:, ls]
        kmod = k * (1.0 + (asig - 1.0) * ka_ref[:, ls])
        ss = _head_sum(kkr * kkr, ones_blk)
        bonus_s = _head_sum(r * kmod * rk_ref[:, ls], ones_blk)
        hi, lo = _split(logd)
        cl = jnp.concatenate(
            [_dot(tri2, jnp.concatenate([hi[c * CHUNK:(c + 1) * CHUNK], lo[c * CHUNK:(c + 1) * CHUNK]], axis=0))
             for c in range(ROW_TILE // CHUNK)], axis=0)
        return ls, r, v, logd, asig, kkr, kmod, ss, bonus_s, cl

    def operands(ls, r, v, logd, asig, kkr, kmod, ss, bonus_s, cl):
        kk = kkr * jnp.minimum(lax.rsqrt(ss), 1.0 / KK_NORM_FLOOR)
        bonus_ref[:, ls] = _bf(bonus_s * v)
        v_ref[:, ls] = _bf(v)
        e_pos = jnp.exp(cl)
        e_neg = jnp.exp(-cl)
        at_ref[:, ls] = _bf(-kk * jnp.exp(cl - logd))
        rt_ref[:, ls] = _bf(r * e_pos)
        bt_ref[:, ls] = _bf(kk * asig * e_neg)
        kt_ref[:, ls] = _bf(kmod * e_neg)
        for c in range(ROW_TILE // CHUNK):
            dl_ref[c:c + 1, ls] = e_pos[(c + 1) * CHUNK - 1:(c + 1) * CHUNK]

    xl = project(3 * w)
    rkv = frames(0)
    lora_in = jnp.where(lane_l < DECAY_LORA, jnp.tanh(xl),
                        jnp.where(lane_l < DECAY_LORA + AAA_LORA, xl, _sigmoid(xl)))
    lora = _dot(_bf(lora_in), wl_ref[...])
    for j in range(groups):
        project_sb(next(sb_slices))
        nxt = frames(j + 1) if j + 1 < groups else None
        staged = sums(j, rkv, lora)
        project_sb(next(sb_slices))
        operands(*staged)
        rkv = nxt
    for lo_ in sb_slices:
        project_sb(lo_)


def _inproj(x2d, g, mu, w_bf, w_lora, w0, a0, k_k, k_a, r_k, seq_len):
    n = x2d.shape[0]
    chunks_per_tile = ROW_TILE // CHUNK
    const = lambda shape: pl.BlockSpec(shape, lambda i: (0, 0))
    wide = lambda: pl.BlockSpec((ROW_TILE, RWKV_WIDTH), lambda i: (i, 0))
    return pl.pallas_call(
        functools.partial(_inproj_kernel, tiles_per_seq=seq_len // ROW_TILE),
        grid=(n // ROW_TILE,),
        in_specs=[
            pl.BlockSpec((ROW_TILE, D_MODEL), lambda i: (i, 0)),
            const((1, D_MODEL)), const((1, RWKV_COLS)), const((D_MODEL, IN_COLS)),
            const((LORA_IN, 3 * RWKV_WIDTH)),
            const((1, RWKV_WIDTH)), const((1, RWKV_WIDTH)), const((1, RWKV_WIDTH)), const((1, RWKV_WIDTH)),
            const((1, RWKV_WIDTH)),
        ],
        out_specs=[wide(), wide(), wide(), wide(), wide(), wide(), wide(),
                   pl.BlockSpec((chunks_per_tile, RWKV_WIDTH), lambda i: (i, 0)),
                   pl.BlockSpec((ROW_TILE, SB_COLS), lambda i: (i, 0))],
        out_shape=[jax.ShapeDtypeStruct((n, RWKV_WIDTH), BF16)] * 7 + [
            jax.ShapeDtypeStruct((n // CHUNK, RWKV_WIDTH), F32),
            jax.ShapeDtypeStruct((n, SB_COLS), BF16),
        ],
        scratch_shapes=[pltpu.VMEM((1, RWKV_COLS), F32)],
        compiler_params=pltpu.CompilerParams(
            dimension_semantics=("arbitrary",), vmem_limit_bytes=VMEM_LIMIT),
        name="inproj",
    )(x2d, g, mu, w_bf, w_lora, w0, a0, k_k, k_a, r_k)


def _rwkv_chains(ch, m):
    def blockdiag(y):
        yb = _bf(y)
        zero = jnp.zeros_like(yb)
        return jnp.concatenate([jnp.where(m["lane_lo"], yb, zero), jnp.where(m["lane_lo"], zero, yb)], axis=0)

    def each(fn, *lists):
        return [fn(*xs) for xs in zip(*lists)]

    at, rt, bt, kt = ([c[k] for c in ch] for k in ("at", "rt", "bt", "kt"))
    v, dl, s = ([c[k] for c in ch] for k in ("v", "dl", "s"))
    g = each(lambda a, r, b, k: _dot_nt(jnp.concatenate([a, r], axis=0),
                                        jnp.concatenate([blockdiag(b), blockdiag(k)], axis=0)),
             at, rt, bt, kt)
    a_ab = [jnp.where(m["strict"], x[:CHUNK, :LANES], 0.0) for x in g]
    a_ak = [jnp.where(m["strict"], x[:CHUNK, LANES:], 0.0) for x in g]
    a_rb = [jnp.where(m["incl"], x[CHUNK:, :LANES], 0.0) for x in g]
    a_rk = [jnp.where(m["incl"], x[CHUNK:, LANES:], 0.0) for x in g]

    t = [m["eye"] + a for a in a_ab]
    x = each(lambda a: _dot(_bf(a), blockdiag(a)), a_ab)
    for _ in range(4):
        tx = each(lambda ti, xi: _dot(_bf(jnp.concatenate([ti, xi], axis=0)), blockdiag(xi)), t, x)
        t = each(lambda ti, txi: ti + txi[:CHUNK], t, tx)
        x = [txi[CHUNK:] for txi in tx]
    t = each(lambda ti, xi: ti + _dot(_bf(ti), blockdiag(xi)), t, x)

    v_bd = [blockdiag(vi) for vi in v]
    av = each(lambda a, vb: _dot(_bf(a), vb), a_ak, v_bd)
    wu = each(lambda ti, a, avi: _dot(_bf(ti), jnp.concatenate([blockdiag(a), blockdiag(avi)], axis=1)),
              t, at, av)
    w = [x_[:, :LANES] for x_ in wu]
    u0 = [x_[:, LANES:] for x_ in wu]
    ry = each(lambda a, wi, ui: _dot(_bf(a), jnp.concatenate([blockdiag(wi), blockdiag(ui)], axis=1)),
              a_rb, w, u0)
    yk = each(lambda a, vb: _dot(_bf(a), vb), a_rk, v_bd)
    rhat = each(lambda r, x_: r.astype(F32) + x_[:, :LANES], rt, ry)
    y0 = each(lambda x_, k_: x_[:, LANES:] + k_, ry, yk)

    def transition(b, k, d, wi, ui, vi):
        bk_t = jnp.concatenate([b.astype(F32) * d, k.astype(F32) * d], axis=0).T
        rhs = jnp.concatenate(
            [jnp.concatenate([_bf(wi), _bf(ui)], axis=1),
             jnp.concatenate([jnp.zeros_like(vi), vi], axis=1)], axis=0)
        return _dot(_bf(bk_t), rhs)

    mn = each(transition, bt, kt, dl, w, u0, v)
    diag_blocks = lambda x_: jnp.where(m["lane_lo"], x_[:CHUNK], x_[CHUNK:])
    m_pair = each(lambda x_, d: diag_blocks(x_[:, :LANES]) + m["eye"] * d, mn, dl)
    n_pair = [diag_blocks(x_[:, LANES:]) for x_ in mn]
    rs = each(lambda r, mp, si: _dot(_bf(jnp.concatenate([r, mp], axis=0)), blockdiag(si)), rhat, m_pair, s)
    y = each(lambda x_, y_: x_[:CHUNK] + y_, rs, y0)
    s_new = each(lambda x_, n_: x_[CHUNK:] + n_, rs, n_pair)
    return y, s_new


def _rwkv_kernel(at_ref, rt_ref, bt_ref, kt_ref, v_ref, gate_ref, bonus_ref, dl_ref, gw_ref, gb_ref,
                 y_ref, state_ref, *, seqs):
    @pl.when(pl.program_id(1) == 0)
    def _():
        state_ref[...] = jnp.zeros_like(state_ref)

    row_t = _iota((CHUNK, LANES), 0)
    col_i = _iota((CHUNK, LANES), 1) & (HEAD_DIM - 1)
    masks = {
        "lane_lo": _iota((CHUNK, LANES), 1) < HEAD_DIM,
        "strict": row_t > col_i,
        "incl": row_t >= col_i,
        "eye": jnp.where(row_t == col_i, 1.0, 0.0).astype(F32),
    }

    chains = []
    for b in range(seqs):
        for pp in range(PAIRS):
            ls = slice(pp * LANES, (pp + 1) * LANES)
            chains.append(dict(at=at_ref[b, :, ls], rt=rt_ref[b, :, ls], bt=bt_ref[b, :, ls],
                               kt=kt_ref[b, :, ls], v=v_ref[b, :, ls], dl=dl_ref[b, :, ls],
                               s=state_ref[b, pp]))
    y_chains, s_chains = _rwkv_chains(chains, masks)
    for b in range(seqs):
        for pp in range(PAIRS):
            state_ref[b, pp] = s_chains[b * PAIRS + pp]
    y = jnp.concatenate(
        [jnp.concatenate(y_chains[b * PAIRS:(b + 1) * PAIRS], axis=1) for b in range(seqs)], axis=0)

    mean_blk = _head_block_matrix(1.0 / HEAD_DIM, MXU_COLS)
    mean = _head_sum(y, mean_blk)
    yc = y - mean
    var = _head_sum(yc * yc, mean_blk)
    yn = yc * lax.rsqrt(var + GN_EPS) * gw_ref[...] + gb_ref[...]
    bonus = jnp.concatenate([bonus_ref[b] for b in range(seqs)], axis=0).astype(F32)
    gate = jnp.concatenate([gate_ref[b] for b in range(seqs)], axis=0).astype(F32)
    out = _bf((yn + bonus) * gate)
    for b in range(seqs):
        y_ref[b] = out[b * CHUNK:(b + 1) * CHUNK]


def _rwkv(at, rt, bt, kt, v, gate, bonus, dl, gn_w, gn_b):
    bsz, t, _ = at.shape
    seqs = RWKV_SEQS if bsz % RWKV_SEQS == 0 else 1
    vec = lambda c: pl.BlockSpec((1, c), lambda b, j: (0, 0))
    frames = lambda: pl.BlockSpec((seqs, CHUNK, RWKV_WIDTH), lambda b, j: (b, j, 0))
    return pl.pallas_call(
        functools.partial(_rwkv_kernel, seqs=seqs),
        grid=(bsz // seqs, t // CHUNK),
        in_specs=[frames(), frames(), frames(), frames(), frames(), frames(), frames(),
                  pl.BlockSpec((seqs, None, 1, RWKV_WIDTH), lambda b, j: (b, j, 0, 0)),
                  vec(RWKV_WIDTH), vec(RWKV_WIDTH)],
        out_specs=frames(),
        out_shape=jax.ShapeDtypeStruct((bsz, t, RWKV_WIDTH), BF16),
        scratch_shapes=[pltpu.VMEM((seqs, PAIRS, CHUNK, LANES), F32)],
        compiler_params=pltpu.CompilerParams(
            dimension_semantics=("arbitrary", "arbitrary"), vmem_limit_bytes=VMEM_LIMIT),
        name="rwkv7",
    )(at, rt, bt, kt, v, gate, bonus, dl, gn_w, gn_b)


def _sb_kernel(q_ref, k_ref, v_ref, gain_ref, o_ref, c_ref, acc_ref, *, seqs):
    qi = pl.program_id(1)
    half = Q_BLOCK // 2
    units = [(s, u, h) for s in range(seqs) for u in range(PAIRS) for h in range(2)]
    lanes = lambda u: slice(u * LANES, (u + 1) * LANES)
    lane_lo = _iota((half, LANES), 1) < HEAD_DIM
    q2 = []
    for s, u, h in units:
        q = q_ref[s, h * half:(h + 1) * half, lanes(u)]
        zero = jnp.zeros_like(q)
        q2.append(jnp.concatenate([jnp.where(lane_lo, q, zero), jnp.where(lane_lo, zero, q)], axis=0))

    row_q = _iota((Q_BLOCK, LANES), 0) & (half - 1)
    col_k = _iota((Q_BLOCK, LANES), 1)
    neg_cum = jnp.where(
        (_iota((LANES, 2 * LANES), 0) > _iota((LANES, 2 * LANES), 1))
        | (_iota((LANES, 2 * LANES), 1) >= LANES), -1.0, 0.0).astype(BF16)
    q_start = qi * Q_BLOCK

    def softplus2(x):
        return jnp.maximum(x, 0.0) + jnp.log2(1.0 + jnp.exp2(-jnp.abs(x)))

    def tile(starts, visible, first):
        kj = [[k_ref[s, pl.ds(starts[h], Q_BLOCK), :] for h in range(2)] for s in range(seqs)]
        vj = [[v_ref[s, pl.ds(starts[h], Q_BLOCK), :] for h in range(2)] for s in range(seqs)]
        z = [_dot_nt(q2[i], kj[s][h][:, lanes(u)]) for i, (s, u, h) in enumerate(units)]
        sp = [softplus2(x) for x in z]
        cs = [_dot(_bf(x if visible[h] is None else jnp.where(visible[h], x, 0.0)), neg_cum)
              for x, (s, u, h) in zip(sp, units)]
        cmax = None
        for i, (s, u, h) in enumerate(units):
            log_a = z[i] - sp[i] + cs[i][:, :LANES]
            if not first:
                log_a = log_a + c_ref[i]
            att = jnp.exp2(log_a)
            if visible[h] is not None:
                att = jnp.where(visible[h], att, 0.0)
            pv = _dot(_bf(att), vj[s][h][:, lanes(u)])
            if first:
                acc_ref[i] = pv
                c_new = cs[i][:, LANES:]
            else:
                acc_ref[i] += pv
                c_new = c_ref[i] + cs[i][:, LANES:]
            c_ref[i] = c_new
            cmax = c_new if cmax is None else jnp.maximum(cmax, c_new)
        return jnp.max(cmax)

    start0 = [jnp.maximum(q_start + (h - 1) * half, 0) for h in range(2)]
    first_visible = [col_k < row_q + (q_start + h * half - start0[h]) for h in range(2)]
    cmax0 = tile([pl.multiple_of(x, half) for x in start0], first_visible, True)

    def cond(carry):
        i, cmax = carry
        return jnp.logical_and(i < qi, cmax >= EXP2_UNDERFLOW)

    def body(carry):
        i, _ = carry
        starts = [pl.multiple_of(q_start + (h - 1) * half - i * Q_BLOCK, half) for h in range(2)]
        return i + 1, tile(starts, [None, None], False)

    i_end, cmax_end = lax.while_loop(cond, body, (jnp.int32(1), cmax0))

    @pl.when(jnp.logical_and(jnp.logical_and(qi >= 1, i_end == qi), cmax_end >= EXP2_UNDERFLOW))
    def _():
        tile([0, 0], [col_k < half, None], False)

    mean_blk = _head_block_matrix(1.0 / HEAD_DIM)
    for s in range(seqs):
        rows = []
        for h in range(2):
            acc = [acc_ref[(s * PAIRS + u) * 2 + h] for u in range(PAIRS)]
            rows.append(jnp.concatenate([jnp.where(lane_lo, a[:half], a[half:]) for a in acc], axis=1))
        o = jnp.concatenate(rows, axis=0)
        ms = _head_sum(o * o, mean_blk)
        o_ref[s] = _bf(o * lax.rsqrt(ms + RMS_EPS) * gain_ref[...])


def _sb(psb, gain):
    bsz, t, _ = psb.shape
    seqs = SB_SEQS if bsz % SB_SEQS == 0 else 1
    return pl.pallas_call(
        functools.partial(_sb_kernel, seqs=seqs),
        grid=(bsz // seqs, t // Q_BLOCK),
        in_specs=[
            pl.BlockSpec((seqs, Q_BLOCK, SB_WIDTH), lambda b, i: (b, i, 0)),
            pl.BlockSpec((seqs, t, SB_WIDTH), lambda b, i: (b, 0, 1)),
            pl.BlockSpec((seqs, t, SB_WIDTH), lambda b, i: (b, 0, 2)),
            pl.BlockSpec((1, SB_WIDTH), lambda b, i: (0, 0)),
        ],
        out_specs=pl.BlockSpec((seqs, Q_BLOCK, SB_WIDTH), lambda b, i: (b, i, 0)),
        out_shape=jax.ShapeDtypeStruct((bsz, t, SB_WIDTH), BF16),
        scratch_shapes=[
            pltpu.VMEM((2 * seqs * PAIRS, Q_BLOCK, LANES), F32),
            pltpu.VMEM((2 * seqs * PAIRS, Q_BLOCK, LANES), F32),
        ],
        compiler_params=pltpu.CompilerParams(
            dimension_semantics=("arbitrary", "arbitrary"), vmem_limit_bytes=VMEM_LIMIT),
        name="stickbreak",
    )(psb, psb, psb, gain)


def _out_mlp_kernel(x_ref, ya_ref, yb_ref, wo_ref, g2_ref, wu_ref, wd_ref, gf_ref, o_ref, *, final):
    x1 = (x_ref[...] + _dot(ya_ref[...], wo_ref[:RWKV_WIDTH, :])
          + _dot(yb_ref[...], wo_ref[RWKV_WIDTH:, :]))
    h = _bf(_rms(x1) * g2_ref[...])
    o_ref[...] = x1
    for c in range(D_FF // FF_CHUNK):
        cs = slice(c * FF_CHUNK, (c + 1) * FF_CHUNK)
        u = jnp.maximum(_dot(h, wu_ref[:, cs]), 0.0)
        o_ref[...] += _dot(_bf(u * u), wd_ref[cs, :])
    if final:
        o_ref[...] = _rms(o_ref[...]) * gf_ref[...]


def _out_mlp(x2d, ya, yb, wo, g2, wu, wd, gf, final):
    n = x2d.shape[0]
    const = lambda shape: pl.BlockSpec(shape, lambda i: (0, 0), pipeline_mode=pl.Buffered(1))
    return pl.pallas_call(
        functools.partial(_out_mlp_kernel, final=final),
        grid=(n // MLP_ROW_TILE,),
        in_specs=[
            pl.BlockSpec((MLP_ROW_TILE, D_MODEL), lambda i: (i, 0)),
            pl.BlockSpec((MLP_ROW_TILE, RWKV_WIDTH), lambda i: (i, 0)),
            pl.BlockSpec((MLP_ROW_TILE, SB_WIDTH), lambda i: (i, 0)),
            const((D_MODEL, D_MODEL)),
            const((1, D_MODEL)),
            const((D_MODEL, D_FF)),
            const((D_FF, D_MODEL)),
            const((1, D_MODEL)),
        ],
        out_specs=pl.BlockSpec((MLP_ROW_TILE, D_MODEL), lambda i: (i, 0)),
        out_shape=jax.ShapeDtypeStruct((n, D_MODEL), F32),
        compiler_params=pltpu.CompilerParams(
            dimension_semantics=("arbitrary",), vmem_limit_bytes=VMEM_LIMIT),
        name="outproj_mlp",
    )(x2d, ya, yb, wo, g2, wu, wd, gf)


def _lora_weight(w_decay_up, w_aaa_up, w_gate_up):
    z = lambda r: jnp.zeros((r, RWKV_WIDTH), F32)
    return _bf(jnp.concatenate([
        jnp.concatenate([w_decay_up, z(DECAY_LORA), z(DECAY_LORA)], axis=1),
        jnp.concatenate([z(AAA_LORA), w_aaa_up, z(AAA_LORA)], axis=1),
        jnp.concatenate([z(GATE_LORA), z(GATE_LORA), w_gate_up], axis=1),
    ], axis=0))


def kernel(x, ln1_g, w_in, tok_mu, w0, w_decay_up, a0, w_aaa_up, w_gate_up, k_k, k_a, r_k,
           gn_w, gn_b, sb_gain, w_out, ln2_g, w_up, w_down, lnf_g):
    bsz, t, d = x.shape
    assert d == D_MODEL and t % ROW_TILE == 0 and (bsz * t) % MLP_ROW_TILE == 0
    depth = ln1_g.shape[0]
    row = lambda a: a.reshape(1, -1).astype(F32)
    x2d = x.reshape(bsz * t, d)
    for l in range(depth):
        *frames, dl, psb = _inproj(x2d, row(ln1_g[l]), row(tok_mu[l]), _bf(w_in[l]),
                                   _lora_weight(w_decay_up[l], w_aaa_up[l], w_gate_up[l]),
                                   row(w0[l]), row(a0[l]), row(k_k[l]), row(k_a[l]), row(r_k[l]), t)
        y_rwkv = _rwkv(*[f.reshape(bsz, t, RWKV_WIDTH) for f in frames],
                       dl.reshape(bsz, t // CHUNK, 1, RWKV_WIDTH), row(gn_w[l]), row(gn_b[l]))
        y_sb = _sb(psb.reshape(bsz, t, SB_COLS), row(sb_gain[l]))
        x2d = _out_mlp(x2d, y_rwkv.reshape(bsz * t, RWKV_WIDTH), y_sb.reshape(bsz * t, SB_WIDTH),
                       _bf(w_out[l]), row(ln2_g[l]), _bf(w_up[l]), _bf(w_down[l]), row(lnf_g),
                       final=(l == depth - 1))
    return x2d.reshape(bsz, t, d)
```

```python
import functools

import jax
import jax.numpy as jnp
from jax import lax
from jax.experimental import pallas as pl
from jax.experimental.pallas import tpu as pltpu

D_MODEL = 1024
HEAD_DIM = 64
RWKV_WIDTH = 512
SB_WIDTH = 512
DECAY_LORA = 64
AAA_LORA = 64
GATE_LORA = 128
LORA_IN = DECAY_LORA + AAA_LORA + GATE_LORA
RWKV_COLS = 3 * RWKV_WIDTH + LORA_IN
SB_COLS = 3 * SB_WIDTH
IN_COLS = RWKV_COLS + SB_COLS
D_FF = 4 * D_MODEL
RMS_EPS = 1e-5
GN_EPS = 64e-5
KK_NORM_FLOOR = 1e-12
DECAY_SCALE = -0.6065306597126334

LANES = 128
SUBLANES = 8
HEAD_SHIFT = HEAD_DIM.bit_length() - 1
MXU_COLS = 256
PAIRS = RWKV_WIDTH // LANES
CHUNK = 64
Q_BLOCK = 128
ROW_TILE = 512
MLP_ROW_TILE = 1024
FF_CHUNK = 1024
RWKV_SEQS = 8
LOG2E = 1.4426950408889634
Q_SCALE = HEAD_DIM ** -0.5 * LOG2E
EXP2_UNDERFLOW = -151.0
SB_SEQS = 4
VMEM_LIMIT = 56 * 1024 * 1024

F32 = jnp.float32
BF16 = jnp.bfloat16


def _bf(x):
    return x.astype(BF16)


def _dot(a, b):
    return jnp.dot(a, b, preferred_element_type=F32)


def _dot_nt(a, b):
    return lax.dot_general(a, b, (((1,), (1,)), ((), ())), preferred_element_type=F32)


def _split(x):
    hi = _bf(x)
    lo = _bf(x - hi.astype(F32))
    return hi, lo


def _iota(shape, axis):
    return lax.broadcasted_iota(jnp.int32, shape, axis)


def _head_block_matrix(value, width=LANES):
    same = (_iota((width, width), 0) >> HEAD_SHIFT) == (_iota((width, width), 1) >> HEAD_SHIFT)
    return jnp.where(same, value, 0.0).astype(BF16)


def _head_sum(x, block):
    width = block.shape[0]
    cols = [_dot(_bf(x[:, j * width:(j + 1) * width]), block) for j in range(x.shape[1] // width)]
    return cols[0] if len(cols) == 1 else jnp.concatenate(cols, axis=1)


def _sigmoid(x):
    return 1.0 / (1.0 + jnp.exp2(x * -LOG2E))


def _rms(x):
    return x * lax.rsqrt(jnp.mean(x * x, axis=-1, keepdims=True) + RMS_EPS)


def _inproj_kernel(x_ref, g_ref, mu_ref, w_ref, wl_ref, w0_ref, a0_ref, kk_ref, ka_ref, rk_ref,
                   at_ref, rt_ref, bt_ref, kt_ref, v_ref, gate_ref, bonus_ref, dl_ref, psb_ref, carry_ref,
                   *, tiles_per_seq):
    @pl.when(lax.rem(pl.program_id(0), tiles_per_seq) == 0)
    def _():
        carry_ref[...] = jnp.zeros_like(carry_ref)

    h = _bf(_rms(x_ref[...]) * g_ref[...])
    first_row = _iota((SUBLANES, MXU_COLS), 0) == 0

    def project_sb(lo):
        p = _dot(h, w_ref[:, RWKV_COLS + lo:RWKV_COLS + lo + MXU_COLS])
        psb_ref[:, lo:lo + MXU_COLS] = _bf(p * Q_SCALE if lo < SB_WIDTH else p)

    def project(lo):
        cs = slice(lo, lo + MXU_COLS)
        p = _dot(h, w_ref[:, cs])
        rolled = pltpu.roll(p, 1, 0)
        prev = jnp.concatenate(
            [jnp.where(first_row, carry_ref[:, cs], rolled[:SUBLANES]), rolled[SUBLANES:]], axis=0)
        carry_ref[:, cs] = p[ROW_TILE - 1:ROW_TILE]
        return p + mu_ref[:, cs] * (prev - p)

    w = RWKV_WIDTH
    assert LORA_IN == MXU_COLS
    groups = w // MXU_COLS
    sb_slices = iter(range(0, SB_COLS, MXU_COLS))
    ones_blk = _head_block_matrix(1.0, MXU_COLS)
    tri = jnp.where(_iota((CHUNK, CHUNK), 0) >= _iota((CHUNK, CHUNK), 1), 1.0, 0.0).astype(BF16)
    tri2 = jnp.concatenate([tri, tri], axis=1)
    lane_l = _iota((ROW_TILE, LORA_IN), 1)

    def frames(j):
        lo_ = j * MXU_COLS
        return project(lo_), project(w + lo_), project(2 * w + lo_)

    def sums(j, rkv, lora):
        ls = slice(j * MXU_COLS, (j + 1) * MXU_COLS)
        r, k, v = rkv
        logd = DECAY_SCALE * _sigmoid(w0_ref[:, ls] + lora[:, ls])
        asig = _sigmoid(a0_ref[:, ls] + lora[:, w + j * MXU_COLS:w + (j + 1) * MXU_COLS])
        gate_ref[:, ls] = _bf(lora[:, 2 * w + j * MXU_COLS:2 * w + (j + 1) * MXU_COLS])
        kkr = k * kk_ref[:, ls]
        kmod = k * (1.0 + (asig - 1.0) * ka_ref[:, ls])
        ss = _head_sum(kkr * kkr, ones_blk)
        bonus_s = _head_sum(r * kmod * rk_ref[:, ls], ones_blk)
        hi, lo = _split(logd)
        cl = jnp.concatenate(
            [_dot(tri2, jnp.concatenate([hi[c * CHUNK:(c + 1) * CHUNK], lo[c * CHUNK:(c + 1) * CHUNK]], axis=0))
             for c in range(ROW_TILE // CHUNK)], axis=0)
        return ls, r, v, logd, asig, kkr, kmod, ss, bonus_s, cl

    def operands(ls, r, v, logd, asig, kkr, kmod, ss, bonus_s, cl):
        kk = kkr * jnp.minimum(lax.rsqrt(ss), 1.0 / KK_NORM_FLOOR)
        bonus_ref[:, ls] = _bf(bonus_s * v)
        v_ref[:, ls] = _bf(v)
        e_pos = jnp.exp(cl)
        e_neg = jnp.exp(-cl)
        at_ref[:, ls] = _bf(-kk * jnp.exp(cl - logd))
        rt_ref[:, ls] = _bf(r * e_pos)
        bt_ref[:, ls] = _bf(kk * asig * e_neg)
        kt_ref[:, ls] = _bf(kmod * e_neg)
        for c in range(ROW_TILE // CHUNK):
            dl_ref[c:c + 1, ls] = e_pos[(c + 1) * CHUNK - 1:(c + 1) * CHUNK]

    xl = project(3 * w)
    rkv = frames(0)
    lora_in = jnp.where(lane_l < DECAY_LORA, jnp.tanh(xl),
                        jnp.where(lane_l < DECAY_LORA + AAA_LORA, xl, _sigmoid(xl)))
    lora = _dot(_bf(lora_in), wl_ref[...])
    for j in range(groups):
        project_sb(next(sb_slices))
        nxt = frames(j + 1) if j + 1 < groups else None
        staged = sums(j, rkv, lora)
        project_sb(next(sb_slices))
        operands(*staged)
        rkv = nxt
    for lo_ in sb_slices:
        project_sb(lo_)


def _inproj(x2d, g, mu, w_bf, w_lora, w0, a0, k_k, k_a, r_k, seq_len):
    n = x2d.shape[0]
    chunks_per_tile = ROW_TILE // CHUNK
    const = lambda shape: pl.BlockSpec(shape, lambda i: (0, 0))
    wide = lambda: pl.BlockSpec((ROW_TILE, RWKV_WIDTH), lambda i: (i, 0))
    return pl.pallas_call(
        functools.partial(_inproj_kernel, tiles_per_seq=seq_len // ROW_TILE),
        grid=(n // ROW_TILE,),
        in_specs=[
            pl.BlockSpec((ROW_TILE, D_MODEL), lambda i: (i, 0)),
            const((1, D_MODEL)), const((1, RWKV_COLS)), const((D_MODEL, IN_COLS)),
            const((LORA_IN, 3 * RWKV_WIDTH)),
            const((1, RWKV_WIDTH)), const((1, RWKV_WIDTH)), const((1, RWKV_WIDTH)), const((1, RWKV_WIDTH)),
            const((1, RWKV_WIDTH)),
        ],
        out_specs=[wide(), wide(), wide(), wide(), wide(), wide(), wide(),
                   pl.BlockSpec((chunks_per_tile, RWKV_WIDTH), lambda i: (i, 0)),
                   pl.BlockSpec((ROW_TILE, SB_COLS), lambda i: (i, 0))],
        out_shape=[jax.ShapeDtypeStruct((n, RWKV_WIDTH), BF16)] * 7 + [
            jax.ShapeDtypeStruct((n // CHUNK, RWKV_WIDTH), F32),
            jax.ShapeDtypeStruct((n, SB_COLS), BF16),
        ],
        scratch_shapes=[pltpu.VMEM((1, RWKV_COLS), F32)],
        compiler_params=pltpu.CompilerParams(
            dimension_semantics=("arbitrary",), vmem_limit_bytes=VMEM_LIMIT),
        name="inproj",
    )(x2d, g, mu, w_bf, w_lora, w0, a0, k_k, k_a, r_k)


def _rwkv_chains(ch, m):
    def blockdiag(y):
        yb = _bf(y)
        zero = jnp.zeros_like(yb)
        return jnp.concatenate([jnp.where(m["lane_lo"], yb, zero), jnp.where(m["lane_lo"], zero, yb)], axis=0)

    def each(fn, *lists):
        return [fn(*xs) for xs in zip(*lists)]

    at, rt, bt, kt = ([c[k] for c in ch] for k in ("at", "rt", "bt", "kt"))
    v, dl, s = ([c[k] for c in ch] for k in ("v", "dl", "s"))
    g = each(lambda a, r, b, k: _dot_nt(jnp.concatenate([a, r], axis=0),
                                        jnp.concatenate([blockdiag(b), blockdiag(k)], axis=0)),
             at, rt, bt, kt)
    a_ab = [jnp.where(m["strict"], x[:CHUNK, :LANES], 0.0) for x in g]
    a_ak = [jnp.where(m["strict"], x[:CHUNK, LANES:], 0.0) for x in g]
    a_rb = [jnp.where(m["incl"], x[CHUNK:, :LANES], 0.0) for x in g]
    a_rk = [jnp.where(m["incl"], x[CHUNK:, LANES:], 0.0) for x in g]

    t = [m["eye"] + a for a in a_ab]
    x = each(lambda a: _dot(_bf(a), blockdiag(a)), a_ab)
    for _ in range(4):
        tx = each(lambda ti, xi: _dot(_bf(jnp.concatenate([ti, xi], axis=0)), blockdiag(xi)), t, x)
        t = each(lambda ti, txi: ti + txi[:CHUNK], t, tx)
        x = [txi[CHUNK:] for txi in tx]
    t = each(lambda ti, xi: ti + _dot(_bf(ti), blockdiag(xi)), t, x)

    avk = each(lambda a, b, vi: _dot(_bf(jnp.concatenate([a, b], axis=0)), blockdiag(vi)), a_ak, a_rk, v)
    av = [x_[:CHUNK] for x_ in avk]
    yk = [x_[CHUNK:] for x_ in avk]
    wu = each(lambda ti, a, avi: _dot(_bf(ti), jnp.concatenate([blockdiag(a), blockdiag(avi)], axis=1)),
              t, at, av)
    w = [x_[:, :LANES] for x_ in wu]
    u0 = [x_[:, LANES:] for x_ in wu]
    ry = each(lambda a, wi, ui: _dot(_bf(a), jnp.concatenate([blockdiag(wi), blockdiag(ui)], axis=1)),
              a_rb, w, u0)
    rhat = each(lambda r, x_: r.astype(F32) + x_[:, :LANES], rt, ry)
    y0 = each(lambda x_, k_: x_[:, LANES:] + k_, ry, yk)

    def transition(b, k, d, wi, ui, vi):
        bk_t = jnp.concatenate([b.astype(F32) * d, k.astype(F32) * d], axis=0).T
        rhs = jnp.concatenate(
            [jnp.concatenate([_bf(wi), _bf(ui)], axis=1),
             jnp.concatenate([jnp.zeros_like(vi), vi], axis=1)], axis=0)
        return _dot(_bf(bk_t), rhs)

    mn = each(transition, bt, kt, dl, w, u0, v)
    diag_blocks = lambda x_: jnp.where(m["lane_lo"], x_[:CHUNK], x_[CHUNK:])
    m_pair = each(lambda x_, d: diag_blocks(x_[:, :LANES]) + m["eye"] * d, mn, dl)
    n_pair = [diag_blocks(x_[:, LANES:]) for x_ in mn]
    rs = each(lambda r, mp, si: _dot(_bf(jnp.concatenate([r, mp], axis=0)), blockdiag(si)), rhat, m_pair, s)
    y = each(lambda x_, y_: x_[:CHUNK] + y_, rs, y0)
    s_new = each(lambda x_, n_: x_[CHUNK:] + n_, rs, n_pair)
    return y, s_new


def _rwkv_kernel(at_ref, rt_ref, bt_ref, kt_ref, v_ref, gate_ref, bonus_ref, dl_ref, gw_ref, gb_ref,
                 y_ref, state_ref, *, seqs):
    @pl.when(pl.program_id(1) == 0)
    def _():
        state_ref[...] = jnp.zeros_like(state_ref)

    row_t = _iota((CHUNK, LANES), 0)
    col_i = _iota((CHUNK, LANES), 1) & (HEAD_DIM - 1)
    masks = {
        "lane_lo": _iota((CHUNK, LANES), 1) < HEAD_DIM,
        "strict": row_t > col_i,
        "incl": row_t >= col_i,
        "eye": jnp.where(row_t == col_i, 1.0, 0.0).astype(F32),
    }

    chains = []
    for b in range(seqs):
        for pp in range(PAIRS):
            ls = slice(pp * LANES, (pp + 1) * LANES)
            chains.append(dict(at=at_ref[b, :, ls], rt=rt_ref[b, :, ls], bt=bt_ref[b, :, ls],
                               kt=kt_ref[b, :, ls], v=v_ref[b, :, ls], dl=dl_ref[b, :, ls],
                               s=state_ref[b, pp]))
    y_chains, s_chains = _rwkv_chains(chains, masks)
    for b in range(seqs):
        for pp in range(PAIRS):
            state_ref[b, pp] = s_chains[b * PAIRS + pp]
    y = jnp.concatenate(
        [jnp.concatenate(y_chains[b * PAIRS:(b + 1) * PAIRS], axis=1) for b in range(seqs)], axis=0)

    mean_blk = _head_block_matrix(1.0 / HEAD_DIM, MXU_COLS)
    mean = _head_sum(y, mean_blk)
    yc = y - mean
    var = _head_sum(yc * yc, mean_blk)
    yn = yc * lax.rsqrt(var + GN_EPS) * gw_ref[...] + gb_ref[...]
    bonus = jnp.concatenate([bonus_ref[b] for b in range(seqs)], axis=0).astype(F32)
    gate = jnp.concatenate([gate_ref[b] for b in range(seqs)], axis=0).astype(F32)
    out = _bf((yn + bonus) * gate)
    for b in range(seqs):
        y_ref[b] = out[b * CHUNK:(b + 1) * CHUNK]


def _rwkv(at, rt, bt, kt, v, gate, bonus, dl, gn_w, gn_b):
    bsz, t, _ = at.shape
    seqs = RWKV_SEQS if bsz % RWKV_SEQS == 0 else 1
    vec = lambda c: pl.BlockSpec((1, c), lambda b, j: (0, 0))
    frames = lambda: pl.BlockSpec((seqs, CHUNK, RWKV_WIDTH), lambda b, j: (b, j, 0))
    return pl.pallas_call(
        functools.partial(_rwkv_kernel, seqs=seqs),
        grid=(bsz // seqs, t // CHUNK),
        in_specs=[frames(), frames(), frames(), frames(), frames(), frames(), frames(),
                  pl.BlockSpec((seqs, None, 1, RWKV_WIDTH), lambda b, j: (b, j, 0, 0)),
                  vec(RWKV_WIDTH), vec(RWKV_WIDTH)],
        out_specs=frames(),
        out_shape=jax.ShapeDtypeStruct((bsz, t, RWKV_WIDTH), BF16),
        scratch_shapes=[pltpu.VMEM((seqs, PAIRS, CHUNK, LANES), F32)],
        compiler_params=pltpu.CompilerParams(
            dimension_semantics=("arbitrary", "arbitrary"), vmem_limit_bytes=VMEM_LIMIT),
        name="rwkv7",
    )(at, rt, bt, kt, v, gate, bonus, dl, gn_w, gn_b)


def _sb_kernel(q_ref, k_ref, v_ref, gain_ref, o_ref, c_ref, acc_ref, *, seqs):
    qi = pl.program_id(1)
    half = Q_BLOCK // 2
    units = [(s, u, h) for s in range(seqs) for u in range(PAIRS) for h in range(2)]
    lanes = lambda u: slice(u * LANES, (u + 1) * LANES)
    lane_lo = _iota((half, LANES), 1) < HEAD_DIM
    q2 = []
    for s, u, h in units:
        q = q_ref[s, h * half:(h + 1) * half, lanes(u)]
        zero = jnp.zeros_like(q)
        q2.append(jnp.concatenate([jnp.where(lane_lo, q, zero), jnp.where(lane_lo, zero, q)], axis=0))

    row_q = _iota((Q_BLOCK, LANES), 0) & (half - 1)
    col_k = _iota((Q_BLOCK, LANES), 1)
    neg_cum = jnp.where(
        (_iota((LANES, 2 * LANES), 0) > _iota((LANES, 2 * LANES), 1))
        | (_iota((LANES, 2 * LANES), 1) >= LANES), -1.0, 0.0).astype(BF16)
    q_start = qi * Q_BLOCK

    def softplus2(x):
        return jnp.maximum(x, 0.0) + jnp.log2(1.0 + jnp.exp2(-jnp.abs(x)))

    def tile(starts, visible, first):
        kj = [[k_ref[s, pl.ds(starts[h], Q_BLOCK), :] for h in range(2)] for s in range(seqs)]
        vj = [[v_ref[s, pl.ds(starts[h], Q_BLOCK), :] for h in range(2)] for s in range(seqs)]
        z = [_dot_nt(q2[i], kj[s][h][:, lanes(u)]) for i, (s, u, h) in enumerate(units)]
        sp = [softplus2(x) for x in z]
        cs = [_dot(_bf(x if visible[h] is None else jnp.where(visible[h], x, 0.0)), neg_cum)
              for x, (s, u, h) in zip(sp, units)]
        cmax = None
        for i, (s, u, h) in enumerate(units):
            log_a = z[i] - sp[i] + cs[i][:, :LANES]
            if not first:
                log_a = log_a + c_ref[i]
            att = jnp.exp2(log_a)
            if visible[h] is not None:
                att = jnp.where(visible[h], att, 0.0)
            pv = _dot(_bf(att), vj[s][h][:, lanes(u)])
            if first:
                acc_ref[i] = pv
                c_new = cs[i][:, LANES:]
            else:
                acc_ref[i] += pv
                c_new = c_ref[i] + cs[i][:, LANES:]
            c_ref[i] = c_new
            cmax = c_new if cmax is None else jnp.maximum(cmax, c_new)
        return jnp.max(cmax)

    start0 = [jnp.maximum(q_start + (h - 1) * half, 0) for h in range(2)]
    first_visible = [col_k < row_q + (q_start + h * half - start0[h]) for h in range(2)]
    cmax0 = tile([pl.multiple_of(x, half) for x in start0], first_visible, True)

    def cond(carry):
        i, cmax = carry
        return jnp.logical_and(i < qi, cmax >= EXP2_UNDERFLOW)

    def body(carry):
        i, _ = carry
        starts = [pl.multiple_of(q_start + (h - 1) * half - i * Q_BLOCK, half) for h in range(2)]
        return i + 1, tile(starts, [None, None], False)

    i_end, cmax_end = lax.while_loop(cond, body, (jnp.int32(1), cmax0))

    @pl.when(jnp.logical_and(jnp.logical_and(qi >= 1, i_end == qi), cmax_end >= EXP2_UNDERFLOW))
    def _():
        tile([0, 0], [col_k < half, None], False)

    mean_blk = _head_block_matrix(1.0 / HEAD_DIM)
    for s in range(seqs):
        rows = []
        for h in range(2):
            acc = [acc_ref[(s * PAIRS + u) * 2 + h] for u in range(PAIRS)]
            rows.append(jnp.concatenate([jnp.where(lane_lo, a[:half], a[half:]) for a in acc], axis=1))
        o = jnp.concatenate(rows, axis=0)
        ms = _head_sum(o * o, mean_blk)
        o_ref[s] = _bf(o * lax.rsqrt(ms + RMS_EPS) * gain_ref[...])


def _sb(psb, gain):
    bsz, t, _ = psb.shape
    seqs = SB_SEQS if bsz % SB_SEQS == 0 else 1
    return pl.pallas_call(
        functools.partial(_sb_kernel, seqs=seqs),
        grid=(bsz // seqs, t // Q_BLOCK),
        in_specs=[
            pl.BlockSpec((seqs, Q_BLOCK, SB_WIDTH), lambda b, i: (b, i, 0)),
            pl.BlockSpec((seqs, t, SB_WIDTH), lambda b, i: (b, 0, 1)),
            pl.BlockSpec((seqs, t, SB_WIDTH), lambda b, i: (b, 0, 2)),
            pl.BlockSpec((1, SB_WIDTH), lambda b, i: (0, 0)),
        ],
        out_specs=pl.BlockSpec((seqs, Q_BLOCK, SB_WIDTH), lambda b, i: (b, i, 0)),
        out_shape=jax.ShapeDtypeStruct((bsz, t, SB_WIDTH), BF16),
        scratch_shapes=[
            pltpu.VMEM((2 * seqs * PAIRS, Q_BLOCK, LANES), F32),
            pltpu.VMEM((2 * seqs * PAIRS, Q_BLOCK, LANES), F32),
        ],
        compiler_params=pltpu.CompilerParams(
            dimension_semantics=("arbitrary", "arbitrary"), vmem_limit_bytes=VMEM_LIMIT),
        name="stickbreak",
    )(psb, psb, psb, gain)


def _out_mlp_kernel(x_ref, ya_ref, yb_ref, wo_ref, g2_ref, wu_ref, wd_ref, gf_ref, o_ref, *, final):
    x1 = (x_ref[...] + _dot(ya_ref[...], wo_ref[:RWKV_WIDTH, :])
          + _dot(yb_ref[...], wo_ref[RWKV_WIDTH:, :]))
    h = _bf(_rms(x1) * g2_ref[...])
    o_ref[...] = x1
    for c in range(D_FF // FF_CHUNK):
        cs = slice(c * FF_CHUNK, (c + 1) * FF_CHUNK)
        u = jnp.maximum(_dot(h, wu_ref[:, cs]), 0.0)
        o_ref[...] += _dot(_bf(u * u), wd_ref[cs, :])
    if final:
        o_ref[...] = _rms(o_ref[...]) * gf_ref[...]


def _out_mlp(x2d, ya, yb, wo, g2, wu, wd, gf, final):
    n = x2d.shape[0]
    const = lambda shape: pl.BlockSpec(shape, lambda i: (0, 0), pipeline_mode=pl.Buffered(1))
    return pl.pallas_call(
        functools.partial(_out_mlp_kernel, final=final),
        grid=(n // MLP_ROW_TILE,),
        in_specs=[
            pl.BlockSpec((MLP_ROW_TILE, D_MODEL), lambda i: (i, 0)),
            pl.BlockSpec((MLP_ROW_TILE, RWKV_WIDTH), lambda i: (i, 0)),
            pl.BlockSpec((MLP_ROW_TILE, SB_WIDTH), lambda i: (i, 0)),
            const((D_MODEL, D_MODEL)),
            const((1, D_MODEL)),
            const((D_MODEL, D_FF)),
            const((D_FF, D_MODEL)),
            const((1, D_MODEL)),
        ],
        out_specs=pl.BlockSpec((MLP_ROW_TILE, D_MODEL), lambda i: (i, 0)),
        out_shape=jax.ShapeDtypeStruct((n, D_MODEL), F32),
        compiler_params=pltpu.CompilerParams(
            dimension_semantics=("arbitrary",), vmem_limit_bytes=VMEM_LIMIT),
        name="outproj_mlp",
    )(x2d, ya, yb, wo, g2, wu, wd, gf)


def _lora_weight(w_decay_up, w_aaa_up, w_gate_up):
    z = lambda r: jnp.zeros((r, RWKV_WIDTH), F32)
    return _bf(jnp.concatenate([
        jnp.concatenate([w_decay_up, z(DECAY_LORA), z(DECAY_LORA)], axis=1),
        jnp.concatenate([z(AAA_LORA), w_aaa_up, z(AAA_LORA)], axis=1),
        jnp.concatenate([z(GATE_LORA), z(GATE_LORA), w_gate_up], axis=1),
    ], axis=0))


def kernel(x, ln1_g, w_in, tok_mu, w0, w_decay_up, a0, w_aaa_up, w_gate_up, k_k, k_a, r_k,
           gn_w, gn_b, sb_gain, w_out, ln2_g, w_up, w_down, lnf_g):
    bsz, t, d = x.shape
    assert d == D_MODEL and t % ROW_TILE == 0 and (bsz * t) % MLP_ROW_TILE == 0
    depth = ln1_g.shape[0]
    row = lambda a: a.reshape(1, -1).astype(F32)
    x2d = x.reshape(bsz * t, d)
    for l in range(depth):
        *frames, dl, psb = _inproj(x2d, row(ln1_g[l]), row(tok_mu[l]), _bf(w_in[l]),
                                   _lora_weight(w_decay_up[l], w_aaa_up[l], w_gate_up[l]),
                                   row(w0[l]), row(a0[l]), row(k_k[l]), row(k_a[l]), row(r_k[l]), t)
        y_rwkv = _rwkv(*[f.reshape(bsz, t, RWKV_WIDTH) for f in frames],
                       dl.reshape(bsz, t // CHUNK, 1, RWKV_WIDTH), row(gn_w[l]), row(gn_b[l]))
        y_sb = _sb(psb.reshape(bsz, t, SB_COLS), row(sb_gain[l]))
        x2d = _out_mlp(x2d, y_rwkv.reshape(bsz * t, RWKV_WIDTH), y_sb.reshape(bsz * t, SB_WIDTH),
                       _bf(w_out[l]), row(ln2_g[l]), _bf(w_up[l]), _bf(w_down[l]), row(lnf_g),
                       final=(l == depth - 1))
    return x2d.reshape(bsz, t, d)
```

```python
import functools

import jax
import jax.numpy as jnp
from jax import lax
from jax.experimental import pallas as pl
from jax.experimental.pallas import tpu as pltpu

D_MODEL = 1024
HEAD_DIM = 64
RWKV_WIDTH = 512
SB_WIDTH = 512
DECAY_LORA = 64
AAA_LORA = 64
GATE_LORA = 128
LORA_IN = DECAY_LORA + AAA_LORA + GATE_LORA
RWKV_COLS = 3 * RWKV_WIDTH + LORA_IN
SB_COLS = 3 * SB_WIDTH
IN_COLS = RWKV_COLS + SB_COLS
D_FF = 4 * D_MODEL
RMS_EPS = 1e-5
GN_EPS = 64e-5
KK_NORM_FLOOR = 1e-12
DECAY_SCALE = -0.6065306597126334

LANES = 128
SUBLANES = 8
HEAD_SHIFT = HEAD_DIM.bit_length() - 1
MXU_COLS = 256
PAIRS = RWKV_WIDTH // LANES
CHUNK = 64
Q_BLOCK = 128
ROW_TILE = 512
MLP_ROW_TILE = 1024
FF_CHUNK = 1024
RWKV_SEQS = 16
LOG2E = 1.4426950408889634
Q_SCALE = HEAD_DIM ** -0.5 * LOG2E
EXP2_UNDERFLOW = -151.0
SB_SEQS = 4
VMEM_LIMIT = 56 * 1024 * 1024

F32 = jnp.float32
BF16 = jnp.bfloat16


def _bf(x):
    return x.astype(BF16)


def _dot(a, b):
    return jnp.dot(a, b, preferred_element_type=F32)


def _dot_nt(a, b):
    return lax.dot_general(a, b, (((1,), (1,)), ((), ())), preferred_element_type=F32)


def _split(x):
    hi = _bf(x)
    lo = _bf(x - hi.astype(F32))
    return hi, lo


def _iota(shape, axis):
    return lax.broadcasted_iota(jnp.int32, shape, axis)


def _head_block_matrix(value, width=LANES):
    same = (_iota((width, width), 0) >> HEAD_SHIFT) == (_iota((width, width), 1) >> HEAD_SHIFT)
    return jnp.where(same, value, 0.0).astype(BF16)


def _head_sum(x, block):
    width = block.shape[0]
    cols = [_dot(_bf(x[:, j * width:(j + 1) * width]), block) for j in range(x.shape[1] // width)]
    return cols[0] if len(cols) == 1 else jnp.concatenate(cols, axis=1)


def _sigmoid(x):
    return 1.0 / (1.0 + jnp.exp2(x * -LOG2E))


def _rms(x):
    return x * lax.rsqrt(jnp.mean(x * x, axis=-1, keepdims=True) + RMS_EPS)


def _inproj_kernel(x_ref, g_ref, mu_ref, w_ref, wl_ref, w0_ref, a0_ref, kk_ref, ka_ref, rk_ref,
                   at_ref, rt_ref, bt_ref, kt_ref, v_ref, gate_ref, bonus_ref, dl_ref, psb_ref, carry_ref,
                   *, tiles_per_seq):
    @pl.when(lax.rem(pl.program_id(0), tiles_per_seq) == 0)
    def _():
        carry_ref[...] = jnp.zeros_like(carry_ref)

    h = _bf(_rms(x_ref[...]) * g_ref[...])
    first_row = _iota((SUBLANES, MXU_COLS), 0) == 0

    def project_sb(lo):
        p = _dot(h, w_ref[:, RWKV_COLS + lo:RWKV_COLS + lo + MXU_COLS])
        psb_ref[:, lo:lo + MXU_COLS] = _bf(p * Q_SCALE if lo < SB_WIDTH else p)

    def project(lo):
        cs = slice(lo, lo + MXU_COLS)
        p = _dot(h, w_ref[:, cs])
        rolled = pltpu.roll(p, 1, 0)
        prev = jnp.concatenate(
            [jnp.where(first_row, carry_ref[:, cs], rolled[:SUBLANES]), rolled[SUBLANES:]], axis=0)
        carry_ref[:, cs] = p[ROW_TILE - 1:ROW_TILE]
        return p + mu_ref[:, cs] * (prev - p)

    w = RWKV_WIDTH
    assert LORA_IN == MXU_COLS
    groups = w // MXU_COLS
    sb_slices = iter(range(0, SB_COLS, MXU_COLS))
    ones_blk = _head_block_matrix(1.0, MXU_COLS)
    tri = jnp.where(_iota((CHUNK, CHUNK), 0) >= _iota((CHUNK, CHUNK), 1), 1.0, 0.0).astype(BF16)
    tri2 = jnp.concatenate([tri, tri], axis=1)
    lane_l = _iota((ROW_TILE, LORA_IN), 1)

    def frames(j):
        lo_ = j * MXU_COLS
        return project(lo_), project(w + lo_), project(2 * w + lo_)

    def sums(j, rkv, lora):
        ls = slice(j * MXU_COLS, (j + 1) * MXU_COLS)
        r, k, v = rkv
        logd = DECAY_SCALE * _sigmoid(w0_ref[:, ls] + lora[:, ls])
        asig = _sigmoid(a0_ref[:, ls] + lora[:, w + j * MXU_COLS:w + (j + 1) * MXU_COLS])
        gate_ref[:, ls] = _bf(lora[:, 2 * w + j * MXU_COLS:2 * w + (j + 1) * MXU_COLS])
        kkr = k * kk_ref[:, ls]
        kmod = k * (1.0 + (asig - 1.0) * ka_ref[:, ls])
        ss = _head_sum(kkr * kkr, ones_blk)
        bonus_s = _head_sum(r * kmod * rk_ref[:, ls], ones_blk)
        hi, lo = _split(logd)
        cl = jnp.concatenate(
            [_dot(tri2, jnp.concatenate([hi[c * CHUNK:(c + 1) * CHUNK], lo[c * CHUNK:(c + 1) * CHUNK]], axis=0))
             for c in range(ROW_TILE // CHUNK)], axis=0)
        return ls, r, v, logd, asig, kkr, kmod, ss, bonus_s, cl

    def operands(ls, r, v, logd, asig, kkr, kmod, ss, bonus_s, cl):
        kk = kkr * jnp.minimum(lax.rsqrt(ss), 1.0 / KK_NORM_FLOOR)
        bonus_ref[:, ls] = _bf(bonus_s * v)
        v_ref[:, ls] = _bf(v)
        e_pos = jnp.exp(cl)
        e_neg = jnp.exp(-cl)
        at_ref[:, ls] = _bf(-kk * jnp.exp(cl - logd))
        rt_ref[:, ls] = _bf(r * e_pos)
        bt_ref[:, ls] = _bf(kk * asig * e_neg)
        kt_ref[:, ls] = _bf(kmod * e_neg)
        for c in range(ROW_TILE // CHUNK):
            dl_ref[c:c + 1, ls] = e_pos[(c + 1) * CHUNK - 1:(c + 1) * CHUNK]

    xl = project(3 * w)
    rkv = frames(0)
    lora_in = jnp.where(lane_l < DECAY_LORA, jnp.tanh(xl),
                        jnp.where(lane_l < DECAY_LORA + AAA_LORA, xl, _sigmoid(xl)))
    lora = _dot(_bf(lora_in), wl_ref[...])
    for j in range(groups):
        project_sb(next(sb_slices))
        nxt = frames(j + 1) if j + 1 < groups else None
        staged = sums(j, rkv, lora)
        project_sb(next(sb_slices))
        operands(*staged)
        rkv = nxt
    for lo_ in sb_slices:
        project_sb(lo_)


def _inproj(x2d, g, mu, w_bf, w_lora, w0, a0, k_k, k_a, r_k, seq_len):
    n = x2d.shape[0]
    chunks_per_tile = ROW_TILE // CHUNK
    const = lambda shape: pl.BlockSpec(shape, lambda i: (0, 0))
    wide = lambda: pl.BlockSpec((ROW_TILE, RWKV_WIDTH), lambda i: (i, 0))
    return pl.pallas_call(
        functools.partial(_inproj_kernel, tiles_per_seq=seq_len // ROW_TILE),
        grid=(n // ROW_TILE,),
        in_specs=[
            pl.BlockSpec((ROW_TILE, D_MODEL), lambda i: (i, 0)),
            const((1, D_MODEL)), const((1, RWKV_COLS)), const((D_MODEL, IN_COLS)),
            const((LORA_IN, 3 * RWKV_WIDTH)),
            const((1, RWKV_WIDTH)), const((1, RWKV_WIDTH)), const((1, RWKV_WIDTH)), const((1, RWKV_WIDTH)),
            const((1, RWKV_WIDTH)),
        ],
        out_specs=[wide(), wide(), wide(), wide(), wide(), wide(), wide(),
                   pl.BlockSpec((chunks_per_tile, RWKV_WIDTH), lambda i: (i, 0)),
                   pl.BlockSpec((ROW_TILE, SB_COLS), lambda i: (i, 0))],
        out_shape=[jax.ShapeDtypeStruct((n, RWKV_WIDTH), BF16)] * 7 + [
            jax.ShapeDtypeStruct((n // CHUNK, RWKV_WIDTH), F32),
            jax.ShapeDtypeStruct((n, SB_COLS), BF16),
        ],
        scratch_shapes=[pltpu.VMEM((1, RWKV_COLS), F32)],
        compiler_params=pltpu.CompilerParams(
            dimension_semantics=("arbitrary",), vmem_limit_bytes=VMEM_LIMIT),
        name="inproj",
    )(x2d, g, mu, w_bf, w_lora, w0, a0, k_k, k_a, r_k)


def _rwkv_chains(ch, m):
    def blockdiag(y):
        yb = _bf(y)
        zero = jnp.zeros_like(yb)
        return jnp.concatenate([jnp.where(m["lane_lo"], yb, zero), jnp.where(m["lane_lo"], zero, yb)], axis=0)

    def each(fn, *lists):
        return [fn(*xs) for xs in zip(*lists)]

    at, rt, bt, kt = ([c[k] for c in ch] for k in ("at", "rt", "bt", "kt"))
    v, dl, s = ([c[k] for c in ch] for k in ("v", "dl", "s"))
    g = each(lambda a, r, b, k: _dot_nt(jnp.concatenate([a, r], axis=0),
                                        jnp.concatenate([blockdiag(b), blockdiag(k)], axis=0)),
             at, rt, bt, kt)
    a_ab = [jnp.where(m["strict"], x[:CHUNK, :LANES], 0.0) for x in g]
    a_ak = [jnp.where(m["strict"], x[:CHUNK, LANES:], 0.0) for x in g]
    a_rb = [jnp.where(m["incl"], x[CHUNK:, :LANES], 0.0) for x in g]
    a_rk = [jnp.where(m["incl"], x[CHUNK:, LANES:], 0.0) for x in g]

    t = [m["eye"] + a for a in a_ab]
    x = each(lambda a: _dot(_bf(a), blockdiag(a)), a_ab)
    for _ in range(4):
        tx = each(lambda ti, xi: _dot(_bf(jnp.concatenate([ti, xi], axis=0)), blockdiag(xi)), t, x)
        t = each(lambda ti, txi: ti + txi[:CHUNK], t, tx)
        x = [txi[CHUNK:] for txi in tx]
    t = each(lambda ti, xi: ti + _dot(_bf(ti), blockdiag(xi)), t, x)

    avk = each(lambda a, b, vi: _dot(_bf(jnp.concatenate([a, b], axis=0)), blockdiag(vi)), a_ak, a_rk, v)
    av = [x_[:CHUNK] for x_ in avk]
    yk = [x_[CHUNK:] for x_ in avk]
    wu = each(lambda ti, a, avi: _dot(_bf(ti), jnp.concatenate([blockdiag(a), blockdiag(avi)], axis=1)),
              t, at, av)
    w = [x_[:, :LANES] for x_ in wu]
    u0 = [x_[:, LANES:] for x_ in wu]
    ry = each(lambda a, wi, ui: _dot(_bf(a), jnp.concatenate([blockdiag(wi), blockdiag(ui)], axis=1)),
              a_rb, w, u0)
    rhat = each(lambda r, x_: r.astype(F32) + x_[:, :LANES], rt, ry)
    y0 = each(lambda x_, k_: x_[:, LANES:] + k_, ry, yk)

    def transition(b, k, d, wi, ui, vi):
        bk_t = jnp.concatenate([b.astype(F32) * d, k.astype(F32) * d], axis=0).T
        rhs = jnp.concatenate(
            [jnp.concatenate([_bf(wi), _bf(ui)], axis=1),
             jnp.concatenate([jnp.zeros_like(vi), vi], axis=1)], axis=0)
        return _dot(_bf(bk_t), rhs)

    mn = each(transition, bt, kt, dl, w, u0, v)
    diag_blocks = lambda x_: jnp.where(m["lane_lo"], x_[:CHUNK], x_[CHUNK:])
    m_pair = each(lambda x_, d: diag_blocks(x_[:, :LANES]) + m["eye"] * d, mn, dl)
    n_pair = [diag_blocks(x_[:, LANES:]) for x_ in mn]
    rs = each(lambda r, mp, si: _dot(_bf(jnp.concatenate([r, mp], axis=0)), blockdiag(si)), rhat, m_pair, s)
    y = each(lambda x_, y_: x_[:CHUNK] + y_, rs, y0)
    s_new = each(lambda x_, n_: x_[CHUNK:] + n_, rs, n_pair)
    return y, s_new


def _rwkv_kernel(at_ref, rt_ref, bt_ref, kt_ref, v_ref, gate_ref, bonus_ref, dl_ref, gw_ref, gb_ref,
                 y_ref, state_ref, *, seqs):
    @pl.when(pl.program_id(1) == 0)
    def _():
        state_ref[...] = jnp.zeros_like(state_ref)

    row_t = _iota((CHUNK, LANES), 0)
    col_i = _iota((CHUNK, LANES), 1) & (HEAD_DIM - 1)
    masks = {
        "lane_lo": _iota((CHUNK, LANES), 1) < HEAD_DIM,
        "strict": row_t > col_i,
        "incl": row_t >= col_i,
        "eye": jnp.where(row_t == col_i, 1.0, 0.0).astype(F32),
    }

    chains = []
    for b in range(seqs):
        for pp in range(PAIRS):
            ls = slice(pp * LANES, (pp + 1) * LANES)
            chains.append(dict(at=at_ref[b, :, ls], rt=rt_ref[b, :, ls], bt=bt_ref[b, :, ls],
                               kt=kt_ref[b, :, ls], v=v_ref[b, :, ls], dl=dl_ref[b, :, ls],
                               s=state_ref[b, pp]))
    y_chains, s_chains = _rwkv_chains(chains, masks)
    for b in range(seqs):
        for pp in range(PAIRS):
            state_ref[b, pp] = s_chains[b * PAIRS + pp]
    y = jnp.concatenate(
        [jnp.concatenate(y_chains[b * PAIRS:(b + 1) * PAIRS], axis=1) for b in range(seqs)], axis=0)

    mean_blk = _head_block_matrix(1.0 / HEAD_DIM, MXU_COLS)
    mean = _head_sum(y, mean_blk)
    yc = y - mean
    var = _head_sum(yc * yc, mean_blk)
    yn = yc * lax.rsqrt(var + GN_EPS) * gw_ref[...] + gb_ref[...]
    bonus = jnp.concatenate([bonus_ref[b] for b in range(seqs)], axis=0).astype(F32)
    gate = jnp.concatenate([gate_ref[b] for b in range(seqs)], axis=0).astype(F32)
    out = _bf((yn + bonus) * gate)
    for b in range(seqs):
        y_ref[b] = out[b * CHUNK:(b + 1) * CHUNK]


def _rwkv(at, rt, bt, kt, v, gate, bonus, dl, gn_w, gn_b):
    bsz, t, _ = at.shape
    seqs = RWKV_SEQS if bsz % RWKV_SEQS == 0 else 1
    vec = lambda c: pl.BlockSpec((1, c), lambda b, j: (0, 0))
    frames = lambda: pl.BlockSpec((seqs, CHUNK, RWKV_WIDTH), lambda b, j: (b, j, 0))
    return pl.pallas_call(
        functools.partial(_rwkv_kernel, seqs=seqs),
        grid=(bsz // seqs, t // CHUNK),
        in_specs=[frames(), frames(), frames(), frames(), frames(), frames(), frames(),
                  pl.BlockSpec((seqs, None, 1, RWKV_WIDTH), lambda b, j: (b, j, 0, 0)),
                  vec(RWKV_WIDTH), vec(RWKV_WIDTH)],
        out_specs=frames(),
        out_shape=jax.ShapeDtypeStruct((bsz, t, RWKV_WIDTH), BF16),
        scratch_shapes=[pltpu.VMEM((seqs, PAIRS, CHUNK, LANES), F32)],
        compiler_params=pltpu.CompilerParams(
            dimension_semantics=("arbitrary", "arbitrary"), vmem_limit_bytes=VMEM_LIMIT),
        name="rwkv7",
    )(at, rt, bt, kt, v, gate, bonus, dl, gn_w, gn_b)


def _sb_kernel(q_ref, k_ref, v_ref, gain_ref, o_ref, c_ref, acc_ref, *, seqs):
    qi = pl.program_id(1)
    half = Q_BLOCK // 2
    units = [(s, u, h) for s in range(seqs) for u in range(PAIRS) for h in range(2)]
    lanes = lambda u: slice(u * LANES, (u + 1) * LANES)
    lane_lo = _iota((half, LANES), 1) < HEAD_DIM
    q2 = []
    for s, u, h in units:
        q = q_ref[s, h * half:(h + 1) * half, lanes(u)]
        zero = jnp.zeros_like(q)
        q2.append(jnp.concatenate([jnp.where(lane_lo, q, zero), jnp.where(lane_lo, zero, q)], axis=0))

    row_q = _iota((Q_BLOCK, LANES), 0) & (half - 1)
    col_k = _iota((Q_BLOCK, LANES), 1)
    neg_cum = jnp.where(
        (_iota((LANES, 2 * LANES), 0) > _iota((LANES, 2 * LANES), 1))
        | (_iota((LANES, 2 * LANES), 1) >= LANES), -1.0, 0.0).astype(BF16)
    q_start = qi * Q_BLOCK

    def softplus2(x):
        return jnp.maximum(x, 0.0) + jnp.log2(1.0 + jnp.exp2(-jnp.abs(x)))

    def tile(starts, visible, first):
        kj = [[k_ref[s, pl.ds(starts[h], Q_BLOCK), :] for h in range(2)] for s in range(seqs)]
        vj = [[v_ref[s, pl.ds(starts[h], Q_BLOCK), :] for h in range(2)] for s in range(seqs)]
        z = [_dot_nt(q2[i], kj[s][h][:, lanes(u)]) for i, (s, u, h) in enumerate(units)]
        sp = [softplus2(x) for x in z]
        cs = [_dot(_bf(x if visible[h] is None else jnp.where(visible[h], x, 0.0)), neg_cum)
              for x, (s, u, h) in zip(sp, units)]
        cmax = None
        for i, (s, u, h) in enumerate(units):
            log_a = z[i] - sp[i] + cs[i][:, :LANES]
            if not first:
                log_a = log_a + c_ref[i]
            att = jnp.exp2(log_a)
            if visible[h] is not None:
                att = jnp.where(visible[h], att, 0.0)
            pv = _dot(_bf(att), vj[s][h][:, lanes(u)])
            if first:
                acc_ref[i] = pv
                c_new = cs[i][:, LANES:]
            else:
                acc_ref[i] += pv
                c_new = c_ref[i] + cs[i][:, LANES:]
            c_ref[i] = c_new
            cmax = c_new if cmax is None else jnp.maximum(cmax, c_new)
        return jnp.max(cmax)

    start0 = [jnp.maximum(q_start + (h - 1) * half, 0) for h in range(2)]
    first_visible = [col_k < row_q + (q_start + h * half - start0[h]) for h in range(2)]
    cmax0 = tile([pl.multiple_of(x, half) for x in start0], first_visible, True)

    def cond(carry):
        i, cmax = carry
        return jnp.logical_and(i < qi, cmax >= EXP2_UNDERFLOW)

    def body(carry):
        i, _ = carry
        starts = [pl.multiple_of(q_start + (h - 1) * half - i * Q_BLOCK, half) for h in range(2)]
        return i + 1, tile(starts, [None, None], False)

    i_end, cmax_end = lax.while_loop(cond, body, (jnp.int32(1), cmax0))

    @pl.when(jnp.logical_and(jnp.logical_and(qi >= 1, i_end == qi), cmax_end >= EXP2_UNDERFLOW))
    def _():
        tile([0, 0], [col_k < half, None], False)

    mean_blk = _head_block_matrix(1.0 / HEAD_DIM)
    for s in range(seqs):
        rows = []
        for h in range(2):
            acc = [acc_ref[(s * PAIRS + u) * 2 + h] for u in range(PAIRS)]
            rows.append(jnp.concatenate([jnp.where(lane_lo, a[:half], a[half:]) for a in acc], axis=1))
        o = jnp.concatenate(rows, axis=0)
        ms = _head_sum(o * o, mean_blk)
        o_ref[s] = _bf(o * lax.rsqrt(ms + RMS_EPS) * gain_ref[...])


def _sb(psb, gain):
    bsz, t, _ = psb.shape
    seqs = SB_SEQS if bsz % SB_SEQS == 0 else 1
    return pl.pallas_call(
        functools.partial(_sb_kernel, seqs=seqs),
        grid=(bsz // seqs, t // Q_BLOCK),
        in_specs=[
            pl.BlockSpec((seqs, Q_BLOCK, SB_WIDTH), lambda b, i: (b, i, 0)),
            pl.BlockSpec((seqs, t, SB_WIDTH), lambda b, i: (b, 0, 1)),
            pl.BlockSpec((seqs, t, SB_WIDTH), lambda b, i: (b, 0, 2)),
            pl.BlockSpec((1, SB_WIDTH), lambda b, i: (0, 0)),
        ],
        out_specs=pl.BlockSpec((seqs, Q_BLOCK, SB_WIDTH), lambda b, i: (b, i, 0)),
        out_shape=jax.ShapeDtypeStruct((bsz, t, SB_WIDTH), BF16),
        scratch_shapes=[
            pltpu.VMEM((2 * seqs * PAIRS, Q_BLOCK, LANES), F32),
            pltpu.VMEM((2 * seqs * PAIRS, Q_BLOCK, LANES), F32),
        ],
        compiler_params=pltpu.CompilerParams(
            dimension_semantics=("arbitrary", "arbitrary"), vmem_limit_bytes=VMEM_LIMIT),
        name="stickbreak",
    )(psb, psb, psb, gain)


def _out_mlp_kernel(x_ref, ya_ref, yb_ref, wo_ref, g2_ref, wu_ref, wd_ref, gf_ref, o_ref, *, final):
    x1 = x_ref[...] + _dot(jnp.concatenate([ya_ref[...], yb_ref[...]], axis=1), wo_ref[...])
    h = _bf(_rms(x1) * g2_ref[...])
    o_ref[...] = x1
    for c in range(D_FF // FF_CHUNK):
        cs = slice(c * FF_CHUNK, (c + 1) * FF_CHUNK)
        u = jnp.maximum(_dot(h, wu_ref[:, cs]), 0.0)
        o_ref[...] += _dot(_bf(u * u), wd_ref[cs, :])
    if final:
        o_ref[...] = _rms(o_ref[...]) * gf_ref[...]


def _out_mlp(x2d, ya, yb, wo, g2, wu, wd, gf, final):
    n = x2d.shape[0]
    const = lambda shape: pl.BlockSpec(shape, lambda i: (0, 0), pipeline_mode=pl.Buffered(1))
    return pl.pallas_call(
        functools.partial(_out_mlp_kernel, final=final),
        grid=(n // MLP_ROW_TILE,),
        in_specs=[
            pl.BlockSpec((MLP_ROW_TILE, D_MODEL), lambda i: (i, 0)),
            pl.BlockSpec((MLP_ROW_TILE, RWKV_WIDTH), lambda i: (i, 0)),
            pl.BlockSpec((MLP_ROW_TILE, SB_WIDTH), lambda i: (i, 0)),
            const((D_MODEL, D_MODEL)),
            const((1, D_MODEL)),
            const((D_MODEL, D_FF)),
            const((D_FF, D_MODEL)),
            const((1, D_MODEL)),
        ],
        out_specs=pl.BlockSpec((MLP_ROW_TILE, D_MODEL), lambda i: (i, 0)),
        out_shape=jax.ShapeDtypeStruct((n, D_MODEL), F32),
        compiler_params=pltpu.CompilerParams(
            dimension_semantics=("arbitrary",), vmem_limit_bytes=VMEM_LIMIT),
        name="outproj_mlp",
    )(x2d, ya, yb, wo, g2, wu, wd, gf)


def _lora_weight(w_decay_up, w_aaa_up, w_gate_up):
    z = lambda r: jnp.zeros((r, RWKV_WIDTH), F32)
    return _bf(jnp.concatenate([
        jnp.concatenate([w_decay_up, z(DECAY_LORA), z(DECAY_LORA)], axis=1),
        jnp.concatenate([z(AAA_LORA), w_aaa_up, z(AAA_LORA)], axis=1),
        jnp.concatenate([z(GATE_LORA), z(GATE_LORA), w_gate_up], axis=1),
    ], axis=0))


def kernel(x, ln1_g, w_in, tok_mu, w0, w_decay_up, a0, w_aaa_up, w_gate_up, k_k, k_a, r_k,
           gn_w, gn_b, sb_gain, w_out, ln2_g, w_up, w_down, lnf_g):
    bsz, t, d = x.shape
    assert d == D_MODEL and t % ROW_TILE == 0 and (bsz * t) % MLP_ROW_TILE == 0
    depth = ln1_g.shape[0]
    row = lambda a: a.reshape(1, -1).astype(F32)
    x2d = x.reshape(bsz * t, d)
    for l in range(depth):
        *frames, dl, psb = _inproj(x2d, row(ln1_g[l]), row(tok_mu[l]), _bf(w_in[l]),
                                   _lora_weight(w_decay_up[l], w_aaa_up[l], w_gate_up[l]),
                                   row(w0[l]), row(a0[l]), row(k_k[l]), row(k_a[l]), row(r_k[l]), t)
        y_rwkv = _rwkv(*[f.reshape(bsz, t, RWKV_WIDTH) for f in frames],
                       dl.reshape(bsz, t // CHUNK, 1, RWKV_WIDTH), row(gn_w[l]), row(gn_b[l]))
        y_sb = _sb(psb.reshape(bsz, t, SB_COLS), row(sb_gain[l]))
        x2d = _out_mlp(x2d, y_rwkv.reshape(bsz * t, RWKV_WIDTH), y_sb.reshape(bsz * t, SB_WIDTH),
                       _bf(w_out[l]), row(ln2_g[l]), _bf(w_up[l]), _bf(w_down[l]), row(lnf_g),
                       final=(l == depth - 1))
    return x2d.reshape(bsz, t, d)
```
